```python
import math
import jax, jax.numpy as jnp
from jax import lax
import numpy as np

D_MODEL = 1024
BATCH = 4
SEQ = 8192
DEPTH = 4

N_MIXERS = 4
EPS = 1e-6
Q_BLOCK = 128
CHUNK = 128

MLA_HEADS = 8
MLA_NOPE = 128
MLA_ROPE = 64
MLA_V = 128
MLA_Q_LORA = 384
MLA_KV_LORA = 256
ROPE_THETA = 10000.0

DIFF_HEADS = 8
DIFF_HD = D_MODEL // (2 * DIFF_HEADS)

MLSTM_HEADS = 8
MLSTM_QK = D_MODEL // 2 // MLSTM_HEADS
MLSTM_V = D_MODEL // MLSTM_HEADS

RET_HEADS = D_MODEL // 256
RET_QK = 256
RET_V = 2 * D_MODEL // RET_HEADS

FFN_HIDDEN = 2816
CONV_W = 3

kernel_name = 'hybrid_interleaved_encoder'


def n_of(kind):
    return len(range(kind, DEPTH, N_MIXERS))


def rms_norm(x, g):
    xf = x.astype(jnp.float32)
    y = xf * lax.rsqrt(jnp.mean(xf * xf, -1, keepdims=True) + EPS)
    return (y * g.astype(jnp.float32)).astype(x.dtype)


def head_group_norm(x, g):
    xf = x.astype(jnp.float32)
    mu = jnp.mean(xf, -1, keepdims=True)
    xc = xf - mu
    y = xc * lax.rsqrt(jnp.mean(xc * xc, -1, keepdims=True) + EPS)
    return (y * g.astype(jnp.float32)).astype(x.dtype)


def query_blocks(t):
    b, s = t.shape[:2]
    return jnp.moveaxis(t.reshape(b, s // Q_BLOCK, Q_BLOCK, *t.shape[2:]), 1, 0)


def merge_blocks(t):
    t = jnp.moveaxis(t, 0, 1)
    return t.reshape(t.shape[0], -1, *t.shape[3:])


def rope_cos_sin(s, dim):
    inv = ROPE_THETA ** (-jnp.arange(0, dim, 2, dtype=jnp.float32) / dim)
    ang = jnp.arange(s, dtype=jnp.float32)[:, None] * inv[None, :]
    return jnp.cos(ang), jnp.sin(ang)


def apply_rope(x, cos, sin):
    half = x.shape[-1] // 2
    x1, x2 = x[..., :half], x[..., half:]
    cos = cos.astype(x.dtype)
    sin = sin.astype(x.dtype)
    return jnp.concatenate([x1 * cos - x2 * sin, x1 * sin + x2 * cos], -1)


def to_chunks(t):
    b, s = t.shape[:2]
    t = t.reshape(b, s // CHUNK, CHUNK, *t.shape[2:])
    return jnp.moveaxis(jnp.moveaxis(t, 1, 0), 3, 2)


def from_chunks(t):
    t = jnp.moveaxis(jnp.moveaxis(t, 2, 3), 0, 1)
    return t.reshape(t.shape[0], -1, *t.shape[3:])


def mla_mixer(x, w_dq, q_norm_g, w_uq, w_dkv, kv_norm_g, w_ukv, w_o):
    b, s, _ = x.shape
    H = MLA_HEADS
    cq = rms_norm(x @ w_dq, q_norm_g)
    q = (cq @ w_uq).reshape(b, s, H, MLA_NOPE + MLA_ROPE)
    ckv_kr = x @ w_dkv
    ckv = rms_norm(ckv_kr[..., :MLA_KV_LORA], kv_norm_g)
    kv = (ckv @ w_ukv).reshape(b, s, H, MLA_NOPE + MLA_V)
    k_nope, v = kv[..., :MLA_NOPE], kv[..., MLA_NOPE:]
    cos, sin = rope_cos_sin(s, MLA_ROPE)
    q_nope = q[..., :MLA_NOPE]
    q_rope = apply_rope(q[..., MLA_NOPE:], cos[:, None, :], sin[:, None, :])
    k_rope = apply_rope(ckv_kr[..., MLA_KV_LORA:], cos, sin)
    scale = (MLA_NOPE + MLA_ROPE) ** -0.5

    def block(args):
        qn, qr = args
        sc = (jnp.einsum('bqhd,bkhd->bhqk', qn, k_nope).astype(jnp.float32)
              + jnp.einsum('bqhr,bkr->bhqk', qr, k_rope).astype(jnp.float32)) * scale
        p = jax.nn.softmax(sc, -1).astype(v.dtype)
        return jnp.einsum('bhqk,bkhd->bqhd', p, v)

    o = merge_blocks(lax.map(block, (query_blocks(q_nope), query_blocks(q_rope))))
    return o.reshape(b, s, H * MLA_V) @ w_o


def diff_mixer(x, w_qkv, lam_p, subln_g, w_o, layer_idx):
    b, s, _ = x.shape
    H, d = DIFF_HEADS, DIFF_HD
    qkv = x @ w_qkv
    q = qkv[..., :2 * H * d].reshape(b, s, H, 2, d)
    k = qkv[..., 2 * H * d:4 * H * d].reshape(b, s, H, 2, d)
    v = qkv[..., 4 * H * d:].reshape(b, s, H, 2 * d)
    lam_init = 0.8 - 0.6 * math.exp(-0.3 * layer_idx)
    lp = lam_p.astype(jnp.float32)
    lam = jnp.exp(jnp.sum(lp[0] * lp[1])) - jnp.exp(jnp.sum(lp[2] * lp[3])) + lam_init
    slopes = 2.0 ** (-8.0 * jnp.arange(1, H + 1, dtype=jnp.float32) / H)
    pos = jnp.arange(s, dtype=jnp.int32)
    scale = d ** -0.5

    def block(args):
        qb, qpos = args
        sc = jnp.einsum('bqhjd,bkhjd->bhjqk', qb, k).astype(jnp.float32) * scale
        dist = jnp.abs(qpos[:, None] - pos[None, :]).astype(jnp.float32)
        sc = sc - slopes[None, :, None, None, None] * dist[None, None, None]
        p = jax.nn.softmax(sc, -1)
        a = (p[:, :, 0] - lam * p[:, :, 1]).astype(v.dtype)
        return jnp.einsum('bhqk,bkhd->bqhd', a, v)

    o = merge_blocks(lax.map(block, (query_blocks(q), pos.reshape(-1, Q_BLOCK))))
    o = rms_norm(o, subln_g) * (1.0 - lam_init)
    return o.reshape(b, s, H * 2 * d) @ w_o


def mlstm_chunkwise(q, k, v, i_pre, log_f):
    b, s, H, dk = q.shape
    dv = v.shape[-1]
    f32 = jnp.float32
    xs = (to_chunks(q.astype(f32)), to_chunks(k.astype(f32)), to_chunks(v.astype(f32)),
          to_chunks(i_pre), to_chunks(log_f))
    tri = jnp.tril(jnp.ones((CHUNK, CHUNK), bool))

    def step(carry, inp):
        C, n, m = carry
        qc, kc, vc, ic, fc = inp
        bcum = jnp.cumsum(fc, -1)
        dmat = jnp.where(tri, bcum[..., :, None] - bcum[..., None, :] + ic[..., None, :], -jnp.inf)
        m_inter = bcum + m[..., None]
        m_t = jnp.maximum(jnp.max(dmat, -1), m_inter)
        w_intra = jnp.exp(dmat - m_t[..., None])
        w_inter = jnp.exp(m_inter - m_t)
        sqk = jnp.einsum('bhtd,bhsd->bhts', qc, kc) * w_intra
        num = (w_inter[..., None] * jnp.einsum('bhtd,bhde->bhte', qc, C)
               + jnp.einsum('bhts,bhse->bhte', sqk, vc))
        den = w_inter * jnp.einsum('bhtd,bhd->bht', qc, n) + jnp.sum(sqk, -1)
        h = num / jnp.maximum(jnp.abs(den), jnp.exp(-m_t))[..., None]
        b_last = bcum[..., -1]
        g_s = b_last[..., None] - bcum + ic
        m_new = jnp.maximum(b_last + m, jnp.max(g_s, -1))
        w_s = jnp.exp(g_s - m_new[..., None])
        decay = jnp.exp(b_last + m - m_new)
        C_new = decay[..., None, None] * C + jnp.einsum('bhs,bhsd,bhse->bhde', w_s, kc, vc)
        n_new = decay[..., None] * n + jnp.einsum('bhs,bhsd->bhd', w_s, kc)
        return (C_new, n_new, m_new), h

    init = (jnp.zeros((b, H, dk, dv), f32), jnp.zeros((b, H, dk), f32), jnp.zeros((b, H), f32))
    _, hs = lax.scan(step, init, xs)
    return from_chunks(hs).astype(v.dtype)


def mlstm_mixer(x, w_in, b_gates, norm_g, w_out):
    b, s, _ = x.shape
    H, dk, dv = MLSTM_HEADS, MLSTM_QK, MLSTM_V
    proj = x @ w_in
    o1, o2, o3, o4 = H * dk, 2 * H * dk, 2 * H * dk + H * dv, 2 * H * dk + 2 * H * dv
    q = proj[..., :o1].reshape(b, s, H, dk)
    k = proj[..., o1:o2].reshape(b, s, H, dk) * (dk ** -0.5)
    v = proj[..., o2:o3].reshape(b, s, H, dv)
    o_gate = jax.nn.sigmoid(proj[..., o3:o4])
    g = (proj[..., o4:] + b_gates).astype(jnp.float32).reshape(b, s, 4, H)
    h_f = mlstm_chunkwise(q, k, v, g[:, :, 0], jax.nn.log_sigmoid(g[:, :, 1]))
    fl = lambda t: jnp.flip(t, 1)
    h_b = fl(mlstm_chunkwise(fl(q), fl(k), fl(v), fl(g[:, :, 2]), jax.nn.log_sigmoid(fl(g[:, :, 3]))))
    h = head_group_norm(h_f + h_b, norm_g.reshape(H, dv))
    return (h.reshape(b, s, H * dv) * o_gate) @ w_out


def retention_chunkwise(q, k, v, log_gamma):
    b, s, H, dk = q.shape
    dv = v.shape[-1]
    f32 = jnp.float32
    idx = jnp.arange(CHUNK, dtype=f32)
    rel = idx[:, None] - idx[None, :]
    dmask = jnp.exp(jnp.maximum(rel, 0.0)[None] * log_gamma[:, None, None]) * (rel >= 0)[None]
    xi = jnp.exp((idx[None] + 1.0) * log_gamma[:, None])
    zeta = jnp.exp((CHUNK - 1.0 - idx[None]) * log_gamma[:, None])
    g_chunk = jnp.exp(CHUNK * log_gamma)
    xs = (to_chunks(q.astype(f32)), to_chunks(k.astype(f32)), to_chunks(v.astype(f32)))

    def step(R, inp):
        qc, kc, vc = inp
        inner = jnp.einsum('bhts,bhse->bhte', jnp.einsum('bhtd,bhsd->bhts', qc, kc) * dmask, vc)
        cross = jnp.einsum('bhtd,bhde->bhte', qc, R) * xi[None, :, :, None]
        R_new = g_chunk[None, :, None, None] * R + jnp.einsum('bhsd,bhse->bhde', kc * zeta[None, :, :, None], vc)
        return R_new, inner + cross

    _, ys = lax.scan(step, jnp.zeros((b, H, dk, dv), f32), xs)
    return from_chunks(ys).astype(v.dtype)


def retention_mixer(x, w_in, decay_logit, norm_g, w_o):
    b, s, _ = x.shape
    H, dk, dv = RET_HEADS, RET_QK, RET_V
    proj = x @ w_in
    o1, o2, o3 = H * dk, 2 * H * dk, 2 * H * dk + H * dv
    q = proj[..., :o1].reshape(b, s, H, dk)
    k = proj[..., o1:o2].reshape(b, s, H, dk) * (dk ** -0.5)
    v = proj[..., o2:o3].reshape(b, s, H, dv)
    gate = proj[..., o3:]
    lg = jax.nn.log_sigmoid(decay_logit.astype(jnp.float32))
    fl = lambda t: jnp.flip(t, 1)
    y = retention_chunkwise(q, k, v, lg[0]) + fl(retention_chunkwise(fl(q), fl(k), fl(v), lg[1]))
    y = head_group_norm(y, norm_g.reshape(H, dv)).reshape(b, s, H * dv)
    return (jax.nn.silu(gate) * y) @ w_o


def conv_ffn(x, w_up, conv_w, conv_b, w_down):
    u = x @ w_up
    s = u.shape[1]
    half = CONV_W // 2
    up = jnp.pad(u, ((0, 0), (half, half), (0, 0)))
    c = conv_b
    for j in range(CONV_W):
        c = c + conv_w[j] * up[:, j:j + s]
    g, val = c[..., :FFN_HIDDEN], c[..., FFN_HIDDEN:]
    return (jax.nn.gelu(g) * val) @ w_down


def setup_inputs(seed: int = 0) -> dict:
    key = jax.random.key(seed)
    ks = iter(jax.random.split(key, 40))
    D = D_MODEL
    nA, nB, nC, nD = n_of(0), n_of(1), n_of(2), n_of(3)

    def w(shape, fan_in):
        return jax.random.normal(next(ks), shape, jnp.float32) * (fan_in ** -0.5)

    def gain(shape):
        return 1.0 + 0.02 * jax.random.normal(next(ks), shape, jnp.float32)

    inp = {}
    inp['x'] = jax.random.normal(next(ks), (BATCH, SEQ, D), jnp.float32)
    inp['norm_g'] = gain((DEPTH, 4, D))
    inp['ffn_w_up'] = w((DEPTH, D, 2 * FFN_HIDDEN), D)
    inp['ffn_conv_w'] = w((DEPTH, CONV_W, 2 * FFN_HIDDEN), CONV_W)
    inp['ffn_conv_b'] = 0.01 * jax.random.normal(next(ks), (DEPTH, 2 * FFN_HIDDEN), jnp.float32)
    inp['ffn_w_down'] = w((DEPTH, FFN_HIDDEN, D), FFN_HIDDEN)
    inp['mla_w_dq'] = w((nA, D, MLA_Q_LORA), D)
    inp['mla_q_norm_g'] = gain((nA, MLA_Q_LORA))
    inp['mla_w_uq'] = w((nA, MLA_Q_LORA, MLA_HEADS * (MLA_NOPE + MLA_ROPE)), MLA_Q_LORA)
    inp['mla_w_dkv'] = w((nA, D, MLA_KV_LORA + MLA_ROPE), D)
    inp['mla_kv_norm_g'] = gain((nA, MLA_KV_LORA))
    inp['mla_w_ukv'] = w((nA, MLA_KV_LORA, MLA_HEADS * (MLA_NOPE + MLA_V)), MLA_KV_LORA)
    inp['mla_w_o'] = w((nA, MLA_HEADS * MLA_V, D), MLA_HEADS * MLA_V)
    inp['diff_w_qkv'] = w((nB, D, 3 * D), D)
    inp['diff_lambda'] = 0.1 * jax.random.normal(next(ks), (nB, 4, DIFF_HD), jnp.float32)
    inp['diff_subln_g'] = gain((nB, 2 * DIFF_HD))
    inp['diff_w_o'] = w((nB, D, D), D)
    H = MLSTM_HEADS
    inp['mlstm_w_in'] = w((nC, D, 2 * H * MLSTM_QK + 2 * H * MLSTM_V + 4 * H), D)
    i_bias = 0.1 * jax.random.normal(next(ks), (nC, 2, H), jnp.float32)
    f_bias = jnp.linspace(3.0, 6.0, H, dtype=jnp.float32)[None, None] + 0.1 * jax.random.normal(next(ks), (nC, 2, H), jnp.float32)
    inp['mlstm_b_gates'] = jnp.stack([i_bias[:, 0], f_bias[:, 0], i_bias[:, 1], f_bias[:, 1]], 1).reshape(nC, 4 * H)
    inp['mlstm_norm_g'] = gain((nC, H * MLSTM_V))
    inp['mlstm_w_out'] = w((nC, H * MLSTM_V, D), H * MLSTM_V)
    Hr = RET_HEADS
    inp['ret_w_in'] = w((nD, D, 2 * Hr * RET_QK + 2 * Hr * RET_V), D)
    eps_h = 2.0 ** (-5.0 - np.arange(Hr, dtype=np.float32))
    base_logit = jnp.asarray(np.log((1.0 - eps_h) / eps_h), jnp.float32)
    inp['ret_decay_logit'] = base_logit[None, None] + 0.1 * jax.random.normal(next(ks), (nD, 2, Hr), jnp.float32)
    inp['ret_norm_g'] = gain((nD, Hr * RET_V))
    inp['ret_w_o'] = w((nD, Hr * RET_V, D), Hr * RET_V)
    return inp


def reference(x, norm_g, ffn_w_up, ffn_conv_w, ffn_conv_b, ffn_w_down,
              mla_w_dq, mla_q_norm_g, mla_w_uq, mla_w_dkv, mla_kv_norm_g, mla_w_ukv, mla_w_o,
              diff_w_qkv, diff_lambda, diff_subln_g, diff_w_o,
              mlstm_w_in, mlstm_b_gates, mlstm_norm_g, mlstm_w_out,
              ret_w_in, ret_decay_logit, ret_norm_g, ret_w_o):
    h = x
    for i in range(DEPTH):
        kind, j = i % N_MIXERS, i // N_MIXERS
        a = rms_norm(h, norm_g[i, 0])
        if kind == 0:
            a = mla_mixer(a, mla_w_dq[j], mla_q_norm_g[j], mla_w_uq[j], mla_w_dkv[j],
                          mla_kv_norm_g[j], mla_w_ukv[j], mla_w_o[j])
        elif kind == 1:
            a = diff_mixer(a, diff_w_qkv[j], diff_lambda[j], diff_subln_g[j], diff_w_o[j], i)
        elif kind == 2:
            a = mlstm_mixer(a, mlstm_w_in[j], mlstm_b_gates[j], mlstm_norm_g[j], mlstm_w_out[j])
        else:
            a = retention_mixer(a, ret_w_in[j], ret_decay_logit[j], ret_norm_g[j], ret_w_o[j])
        h = h + rms_norm(a, norm_g[i, 1])
        f = conv_ffn(rms_norm(h, norm_g[i, 2]), ffn_w_up[i], ffn_conv_w[i], ffn_conv_b[i], ffn_w_down[i])
        h = h + rms_norm(f, norm_g[i, 3])
    return h
```

```python
import functools
import math

import jax
import jax.numpy as jnp
from jax import lax
from jax.experimental import pallas as pl
from jax.experimental.pallas import tpu as pltpu

F32 = jnp.float32
BF16 = jnp.bfloat16

EPS = 1e-6
LOG2E = 1.4426950408889634
CHUNK = 128
HALO = 8

MLA_HEADS, MLA_NOPE, MLA_ROPE, MLA_V = 8, 128, 64, 128
MLA_Q_LORA, MLA_KV_LORA = 384, 256
ROPE_THETA = 10000.0
DIFF_HEADS, DIFF_HD = 8, 64
MLSTM_HEADS, MLSTM_QK, MLSTM_V = 8, 64, 128
RET_HEADS, RET_QK, RET_V = 4, 256, 512
CONV_W = 3

VMEM_LIMIT = 56 * 1024 * 1024


def _params(*sem):
    return pltpu.CompilerParams(dimension_semantics=sem, vmem_limit_bytes=VMEM_LIMIT)


def _resident(shape):
    nd = len(shape)
    return pl.BlockSpec(shape, lambda *_: (0,) * nd, pipeline_mode=pl.Buffered(1))


def _rms(x, g):
    return x * lax.rsqrt(jnp.mean(x * x, axis=-1, keepdims=True) + EPS) * g


def _dot(a, b):
    return jnp.dot(a, b, preferred_element_type=F32)


def _dot_nt(a, b):
    return lax.dot_general(a, b, (((1,), (1,)), ((), ())), preferred_element_type=F32)


def _log_sigmoid(x):
    return jnp.minimum(x, 0.0) - jnp.log1p(jnp.exp(-jnp.abs(x)))


def _sigmoid(x):
    return 1.0 / (1.0 + jnp.exp(-x))


COL_CHUNK = 512


def _proj_body(n_nat, n_t, nat_scales, t_scales, *refs):
    h_ref, g_ref = refs[0], refs[1]
    nat_w = refs[2:2 + n_nat]
    nat_b = refs[2 + n_nat:2 + 2 * n_nat]
    t_w = refs[2 + 2 * n_nat:2 + 2 * n_nat + n_t]
    t_b = refs[2 + 2 * n_nat + n_t:2 + 2 * n_nat + 2 * n_t]
    outs = refs[2 + 2 * n_nat + 2 * n_t:]
    nat_o, t_o = outs[:n_nat], outs[n_nat:]

    xn = _rms(h_ref[0], g_ref[...]).astype(BF16)
    for w_ref, b_ref, o_ref, sc in zip(nat_w, nat_b, nat_o, nat_scales):
        m = w_ref.shape[1]
        for c0 in range(0, m, COL_CHUNK):
            cw = min(COL_CHUNK, m - c0)
            y = _dot(xn, w_ref[:, c0:c0 + cw]) + b_ref[:, c0:c0 + cw]
            if sc != 1.0:
                y = y * sc
            o_ref[0, :, c0:c0 + cw] = y.astype(o_ref.dtype)
    for w_ref, b_ref, o_ref, sc in zip(t_w, t_b, t_o, t_scales):
        m = w_ref.shape[0]
        for r0 in range(0, m, COL_CHUNK):
            rw = min(COL_CHUNK, m - r0)
            y = _dot_nt(w_ref[r0:r0 + rw, :], xn) + b_ref[r0:r0 + rw, :]
            if sc != 1.0:
                y = y * sc
            o_ref[0, r0:r0 + rw, :] = y.astype(o_ref.dtype)


def _norm_proj(h, g, nat, tr, tm):
    B, S, D = h.shape
    args = [h, g.reshape(1, D)]
    specs = [pl.BlockSpec((1, tm, D), lambda b, i: (b, i, 0)), _resident((1, D))]
    for w, _, _, _ in nat:
        args.append(w)
        specs.append(_resident(w.shape))
    for w, bias, _, _ in nat:
        m = w.shape[1]
        args.append(jnp.zeros((1, m), F32) if bias is None else bias.reshape(1, m).astype(F32))
        specs.append(_resident((1, m)))
    for w, _, _, _ in tr:
        args.append(w)
        specs.append(_resident(w.shape))
    for w, bias, _, _ in tr:
        m = w.shape[0]
        args.append(jnp.zeros((m, 1), F32) if bias is None else bias.reshape(m, 1).astype(F32))
        specs.append(_resident((m, 1)))
    out_shape, out_specs = [], []
    for w, _, _, dt in nat:
        m = w.shape[1]
        out_shape.append(jax.ShapeDtypeStruct((B, S, m), dt))
        out_specs.append(pl.BlockSpec((1, tm, m), lambda b, i: (b, i, 0)))
    for w, _, _, dt in tr:
        m = w.shape[0]
        out_shape.append(jax.ShapeDtypeStruct((B, m, S), dt))
        out_specs.append(pl.BlockSpec((1, m, tm), lambda b, i: (b, 0, i)))
    body = functools.partial(_proj_body, len(nat), len(tr),
                             tuple(x[2] for x in nat), tuple(x[2] for x in tr))
    return pl.pallas_call(
        body, grid=(B, S // tm), in_specs=specs, out_specs=out_specs, out_shape=out_shape,
        compiler_params=_params("parallel", "parallel"), name="norm_proj")(*args)


def _out_body(a_ref, w_ref, g_ref, h_ref, o_ref):
    y = _dot(a_ref[0], w_ref[...])
    o_ref[0] = h_ref[0] + _rms(y, g_ref[...])


def _out_proj(a, w, g, h, tm):
    B, S, K = a.shape
    D = h.shape[-1]
    return pl.pallas_call(
        _out_body, grid=(B, S // tm),
        in_specs=[pl.BlockSpec((1, tm, K), lambda b, i: (b, i, 0)), _resident(w.shape),
                  _resident((1, D)), pl.BlockSpec((1, tm, D), lambda b, i: (b, i, 0))],
        out_specs=pl.BlockSpec((1, tm, D), lambda b, i: (b, i, 0)),
        out_shape=jax.ShapeDtypeStruct((B, S, D), F32),
        compiler_params=_params("parallel", "parallel"), name="out_proj")(a, w, g.reshape(1, D), h)


FFN_CHUNK = 256


def _ffn_body(tm, F, hp_ref, hm_ref, hn_ref, g2_ref, wup_ref, cw_ref, cb_ref, wdn_ref, g3_ref,
              o_ref, acc_ref):
    i = pl.program_id(1)
    last = pl.num_programs(1) - 1
    g2 = g2_ref[...]
    hm = hm_ref[0]
    xp = _rms(hp_ref[0], g2) * jnp.where(i == 0, 0.0, 1.0)
    xn_ = _rms(hn_ref[0], g2) * jnp.where(i == last, 0.0, 1.0)
    x_ext = jnp.concatenate([xp, _rms(hm, g2), xn_], axis=0).astype(BF16)

    acc_ref[...] = jnp.zeros_like(acc_ref)

    def chunk(c, carry):
        c0 = pl.multiple_of(c * FFN_CHUNK, FFN_CHUNK)
        halves = []
        for off in (0, F):
            u = _dot(x_ext, wup_ref[:, pl.ds(off + c0, FFN_CHUNK)])
            w = cw_ref[:, pl.ds(off + c0, FFN_CHUNK)]
            b = cb_ref[:, pl.ds(off + c0, FFN_CHUNK)]
            halves.append(b + w[0:1] * u[HALO - 1:HALO - 1 + tm]
                          + w[1:2] * u[HALO:HALO + tm]
                          + w[2:3] * u[HALO + 1:HALO + 1 + tm])
        hid = (jax.nn.gelu(halves[0]) * halves[1]).astype(BF16)
        acc_ref[...] += _dot(hid, wdn_ref[pl.ds(c0, FFN_CHUNK), :])
        return carry

    lax.fori_loop(0, F // FFN_CHUNK, chunk, 0)
    o_ref[0] = hm + _rms(acc_ref[...], g3_ref[...])


def _conv_ffn(h, g2, w_up, conv_w, conv_b, w_down, g3, tm):
    B, S, D = h.shape
    F = w_down.shape[0]
    nb = tm // HALO
    body = functools.partial(_ffn_body, tm, F)
    return pl.pallas_call(
        body, grid=(B, S // tm),
        in_specs=[
            pl.BlockSpec((1, HALO, D), lambda b, i: (b, jnp.maximum(i * nb - 1, 0), 0)),
            pl.BlockSpec((1, tm, D), lambda b, i: (b, i, 0)),
            pl.BlockSpec((1, HALO, D), lambda b, i: (b, jnp.minimum((i + 1) * nb, S // HALO - 1), 0)),
            _resident((1, D)), _resident(w_up.shape), _resident(conv_w.shape),
            _resident((1, 2 * F)), _resident(w_down.shape), _resident((1, D)),
        ],
        out_specs=pl.BlockSpec((1, tm, D), lambda b, i: (b, i, 0)),
        out_shape=jax.ShapeDtypeStruct((B, S, D), F32),
        scratch_shapes=[pltpu.VMEM((tm, D), F32)],
        compiler_params=_params("parallel", "parallel"), name="conv_ffn",
    )(h, h, h, g2.reshape(1, D), w_up, conv_w, conv_b.reshape(1, 2 * F), w_down, g3.reshape(1, D))


def _mla_proj_body(h_ref, g_ref, w1_ref, qg_ref, kvg_ref, wuqT_ref, wuk_ref, wuvT_ref,
                   tab_ref, cosT_ref, sinT_ref, qT_ref, k_ref, vT_ref):
    H = MLA_HEADS
    tm = h_ref.shape[1]
    a = _rms(h_ref[0], g_ref[...]).astype(BF16)
    t1 = _dot(a, w1_ref[...])
    cq = _rms(t1[:, :MLA_Q_LORA], qg_ref[...]).astype(BF16)
    ckv = _rms(t1[:, MLA_Q_LORA:MLA_Q_LORA + MLA_KV_LORA], kvg_ref[...]).astype(BF16)
    z = t1[:, MLA_Q_LORA + MLA_KV_LORA:] * tab_ref[...]
    lane = lax.broadcasted_iota(jnp.int32, z.shape, 1)
    kr = jnp.where(lane < MLA_ROPE, z + pltpu.roll(z, MLA_ROPE, 1), 0.0).astype(BF16)
    k_nope = _dot(ckv, wuk_ref[...]).astype(BF16)
    for hd in range(H):
        k_ref[0, hd] = jnp.concatenate([k_nope[:, hd * MLA_NOPE:(hd + 1) * MLA_NOPE], kr], axis=1)
    vT_ref[0] = _dot_nt(wuvT_ref[...], ckv).astype(BF16)
    scale = (MLA_NOPE + MLA_ROPE) ** -0.5 * LOG2E
    cosT, sinT = cosT_ref[...], sinT_ref[...]
    zeros = jnp.zeros((256 - MLA_NOPE - MLA_ROPE, tm), F32)
    for hd in range(H):
        blk = _dot_nt(wuqT_ref[hd * 256:(hd + 1) * 256, :], cq)
        r = blk[128:192] * cosT + blk[192:256] * sinT
        qT_ref[0, hd * 256:(hd + 1) * 256, :] = (
            jnp.concatenate([blk[:128], r, zeros], axis=0) * scale).astype(BF16)


def _flash_body(tk, qT_ref, k_ref, vT_ref, o_ref):
    S = k_ref.shape[2]
    tq = qT_ref.shape[2]
    dv = vT_ref.shape[1]
    qT = qT_ref[0]

    def step(j, carry):
        m, l, acc = carry
        k0 = pl.multiple_of(j * tk, tk)
        s = _dot(k_ref[0, 0, pl.ds(k0, tk), :], qT)
        m_new = jnp.maximum(m, jnp.max(s, axis=0, keepdims=True))
        p = jnp.exp2(s - m_new)
        alpha = jnp.exp2(m - m_new)
        l = alpha * l + jnp.sum(p, axis=0, keepdims=True)
        acc = alpha * acc + _dot(vT_ref[0, :, pl.ds(k0, tk)], p.astype(BF16))
        return m_new, l, acc

    init = (jnp.full((1, tq), -jnp.inf, F32), jnp.zeros((1, tq), F32), jnp.zeros((dv, tq), F32))
    _, l, acc = lax.fori_loop(0, S // tk, step, init)
    o_ref[0] = (acc / l).T.astype(o_ref.dtype)


def _mla_mixer(h, g0, w_dq, qg, w_uq, w_dkv, kvg, w_ukv, w_o, g1, tm, tq, tk):
    B, S, D = h.shape
    H = MLA_HEADS
    half = MLA_ROPE // 2

    def rot_cols(w):
        return jnp.concatenate([-w[..., half:], w[..., :half]], axis=-1)

    wkr = w_dkv[:, MLA_KV_LORA:]
    w1 = jnp.concatenate([w_dq, w_dkv[:, :MLA_KV_LORA], wkr, rot_cols(wkr)], axis=1).astype(BF16)
    wq = w_uq.reshape(MLA_Q_LORA, H, MLA_NOPE + MLA_ROPE)
    wq_r = wq[..., MLA_NOPE:]
    wuqT = jnp.concatenate([wq[..., :MLA_NOPE], wq_r, rot_cols(wq_r)], axis=-1)
    wuqT = wuqT.reshape(MLA_Q_LORA, H * 256).T.astype(BF16)
    wkv = w_ukv.reshape(MLA_KV_LORA, H, MLA_NOPE + MLA_V)
    wuk = wkv[..., :MLA_NOPE].reshape(MLA_KV_LORA, H * MLA_NOPE).astype(BF16)
    wuvT = wkv[..., MLA_NOPE:].reshape(MLA_KV_LORA, H * MLA_V).T.astype(BF16)

    inv = ROPE_THETA ** (-jnp.arange(0, MLA_ROPE, 2, dtype=F32) / MLA_ROPE)
    ang = jnp.arange(S, dtype=F32)[:, None] * inv[None, :]
    cos, sin = jnp.cos(ang), jnp.sin(ang)
    cos2, sin2 = jnp.concatenate([cos, cos], 1), jnp.concatenate([sin, sin], 1)
    tab = jnp.concatenate([cos2, sin2], 1)

    qT, k, vT = pl.pallas_call(
        _mla_proj_body, grid=(B, S // tm),
        in_specs=[pl.BlockSpec((1, tm, D), lambda b, i: (b, i, 0)), _resident((1, D)),
                  _resident(w1.shape), _resident((1, MLA_Q_LORA)), _resident((1, MLA_KV_LORA)),
                  _resident(wuqT.shape), _resident(wuk.shape), _resident(wuvT.shape),
                  pl.BlockSpec((tm, 128), lambda b, i: (i, 0)),
                  pl.BlockSpec((MLA_ROPE, tm), lambda b, i: (0, i)),
                  pl.BlockSpec((MLA_ROPE, tm), lambda b, i: (0, i))],
        out_specs=[pl.BlockSpec((1, H * 256, tm), lambda b, i: (b, 0, i)),
                   pl.BlockSpec((1, H, tm, 256), lambda b, i: (b, 0, i, 0)),
                   pl.BlockSpec((1, H * MLA_V, tm), lambda b, i: (b, 0, i))],
        out_shape=[jax.ShapeDtypeStruct((B, H * 256, S), BF16),
                   jax.ShapeDtypeStruct((B, H, S, 256), BF16),
                   jax.ShapeDtypeStruct((B, H * MLA_V, S), BF16)],
        compiler_params=_params("parallel", "parallel"), name="mla_proj",
    )(h, g0.reshape(1, D), w1, qg.reshape(1, -1), kvg.reshape(1, -1), wuqT, wuk, wuvT,
      tab, cos2.T, sin2.T)

    o = pl.pallas_call(
        functools.partial(_flash_body, tk), grid=(B, H, S // tq),
        in_specs=[pl.BlockSpec((1, 256, tq), lambda b, hd, i: (b, hd, i)),
                  pl.BlockSpec((1, 1, S, 256), lambda b, hd, i: (b, hd, 0, 0)),
                  pl.BlockSpec((1, MLA_V, S), lambda b, hd, i: (b, hd, 0))],
        out_specs=pl.BlockSpec((1, tq, MLA_V), lambda b, hd, i: (b, i, hd)),
        out_shape=jax.ShapeDtypeStruct((B, S, H * MLA_V), BF16),
        compiler_params=_params("parallel", "parallel", "arbitrary"), name="mla_flash",
    )(qT, k, vT)
    return _out_proj(o, w_o.astype(BF16), g1, h, tm)


def _diff_flash_body(tk, lam_init, slope_ref, qT_ref, k_ref, vT_ref, lamp_ref, sg_ref, o_ref):
    S = k_ref.shape[1]
    tq = qT_ref.shape[2]
    d = DIFF_HD
    hd = pl.program_id(1)
    q0 = pl.program_id(2) * tq
    slope = slope_ref[hd]
    qT = qT_ref[0]
    row = lax.broadcasted_iota(jnp.int32, qT.shape, 0)
    zero = jnp.zeros_like(qT)
    q1, q2 = jnp.where(row < d, qT, zero), jnp.where(row >= d, qT, zero)
    rel = (lax.broadcasted_iota(jnp.int32, (tk, tq), 1) + q0
           - lax.broadcasted_iota(jnp.int32, (tk, tq), 0))

    def step(j, carry):
        m1, l1, a1, m2, l2, a2 = carry
        k0 = pl.multiple_of(j * tk, tk)
        kc = k_ref[0, pl.ds(k0, tk), :]
        vc = vT_ref[0, :, pl.ds(k0, tk)]
        bias = slope * jnp.abs(rel - k0).astype(F32)

        def upd(q, m, l, a):
            s = _dot(kc, q) - bias
            m_new = jnp.maximum(m, jnp.max(s, axis=0, keepdims=True))
            p = jnp.exp2(s - m_new)
            alpha = jnp.exp2(m - m_new)
            return (m_new, alpha * l + jnp.sum(p, axis=0, keepdims=True),
                    alpha * a + _dot(vc, p.astype(BF16)))

        return upd(q1, m1, l1, a1) + upd(q2, m2, l2, a2)

    st = (jnp.full((1, tq), -jnp.inf, F32), jnp.zeros((1, tq), F32), jnp.zeros((2 * d, tq), F32))
    _, l1, a1, _, l2, a2 = lax.fori_loop(0, S // tk, step, st + st)
    lp = lamp_ref[...]
    lam = (jnp.exp(jnp.sum(lp[0:1] * lp[1:2], axis=1, keepdims=True))
           - jnp.exp(jnp.sum(lp[2:3] * lp[3:4], axis=1, keepdims=True)) + lam_init)
    oT = a1 / l1 - lam * (a2 / l2)
    oT = oT * lax.rsqrt(jnp.mean(oT * oT, axis=0, keepdims=True) + EPS)
    oT = oT * (sg_ref[...] * (1.0 - lam_init))
    o_ref[0] = oT.T.astype(o_ref.dtype)


def _diff_mixer(h, g0, w_qkv, lam_p, subln_g, w_o, g1, layer_idx, tm, tq, tk):
    B, S, D = h.shape
    H, d = DIFF_HEADS, DIFF_HD
    scale = d ** -0.5 * LOG2E
    wqT = w_qkv[:, :D].T.astype(BF16)
    wk = w_qkv[:, D:2 * D].astype(BF16)
    wvT = w_qkv[:, 2 * D:].T.astype(BF16)
    k, qT, vT = _norm_proj(h, g0, [(wk, None, 1.0, BF16)],
                           [(wqT, None, scale, BF16), (wvT, None, 1.0, BF16)], tm)
    lam_init = 0.8 - 0.6 * math.exp(-0.3 * layer_idx)
    slopes = (2.0 ** (-8.0 * jnp.arange(1, H + 1, dtype=F32) / H)) * LOG2E
    o = pl.pallas_call(
        functools.partial(_diff_flash_body, tk, lam_init), grid=(B, H, S // tq),
        in_specs=[pl.BlockSpec(memory_space=pltpu.SMEM),
                  pl.BlockSpec((1, 2 * d, tq), lambda b, hd, i: (b, hd, i)),
                  pl.BlockSpec((1, S, 2 * d), lambda b, hd, i: (b, 0, hd)),
                  pl.BlockSpec((1, 2 * d, S), lambda b, hd, i: (b, hd, 0)),
                  pl.BlockSpec((4, d), lambda b, hd, i: (0, 0)),
                  pl.BlockSpec((2 * d, 1), lambda b, hd, i: (0, 0))],
        out_specs=pl.BlockSpec((1, tq, 2 * d), lambda b, hd, i: (b, i, hd)),
        out_shape=jax.ShapeDtypeStruct((B, S, H * 2 * d), BF16),
        compiler_params=_params("parallel", "parallel", "arbitrary"), name="diff_flash",
    )(slopes, qT, k, vT, lam_p.astype(F32), subln_g.reshape(2 * d, 1).astype(F32))
    return _out_proj(o, w_o.astype(BF16), g1, h, tm)


def _mlstm_body(q_ref, kT_ref, v_ref, g_ref, gT_ref, o_ref, c_ref, m_ref):
    H, L, dv = MLSTM_HEADS, CHUNK, MLSTM_V
    sgn = 1 - 2 * pl.program_id(0)

    @pl.when(pl.program_id(2) == 0)
    def _():
        c_ref[...] = jnp.zeros_like(c_ref)
        m_ref[...] = jnp.zeros_like(m_ref)

    r = lax.broadcasted_iota(jnp.int32, (L, L), 0)
    c = lax.broadcasted_iota(jnp.int32, (L, L), 1)
    causal = (r - c) * sgn >= 0
    causal_t = (c - r) * sgn >= 0
    lane = lax.broadcasted_iota(jnp.int32, (L, dv), 1)
    ones_blk = jnp.where(lane == 0, 1.0, 0.0).astype(BF16)
    g = g_ref[0, 0]
    gT = gT_ref[0, 0]
    for hd in range(H):
        i_row = gT[hd:hd + 1, :]
        lf_row = _log_sigmoid(gT[H + hd:H + hd + 1, :])
        lf_col = _log_sigmoid(g[:, H + hd:H + hd + 1])
        bcum_col = jnp.sum(jnp.where(causal, lf_row, 0.0), axis=1, keepdims=True)
        bcum_row = jnp.sum(jnp.where(causal_t, lf_col, 0.0), axis=0, keepdims=True)
        b_last = jnp.sum(lf_row, axis=1, keepdims=True)
        m_prev = m_ref[hd][:, 0:1]

        dmat = jnp.where(causal, bcum_col - bcum_row + i_row, -jnp.inf)
        m_inter = bcum_col + m_prev
        m_t = jnp.maximum(jnp.max(dmat, axis=1, keepdims=True), m_inter)
        w_intra = jnp.exp(dmat - m_t)
        w_inter = jnp.exp(m_inter - m_t)

        qc = q_ref[0, :, hd * 128:(hd + 1) * 128]
        kT = kT_ref[0, hd * 128:(hd + 1) * 128, :]
        v_ext = jnp.concatenate([v_ref[0, :, hd * dv:(hd + 1) * dv], ones_blk], axis=1)
        c_ext = c_ref[hd]
        sqk = (_dot(qc, kT) * w_intra).astype(BF16)
        nd = w_inter * _dot(qc, c_ext.astype(BF16)) + _dot(sqk, v_ext)
        den = nd[:, dv:dv + 1]
        o_ref[0, 0, :, hd * dv:(hd + 1) * dv] = (
            nd[:, :dv] / jnp.maximum(jnp.abs(den), jnp.exp(-m_t)))

        g_s = b_last - bcum_row + i_row
        m_new = jnp.maximum(b_last + m_prev, jnp.max(g_s, axis=1, keepdims=True))
        w_s = jnp.exp(g_s - m_new)
        decay = jnp.exp(b_last + m_prev - m_new)
        c_ref[hd] = decay * c_ext + _dot((kT * w_s).astype(BF16), v_ext)
        m_ref[hd] = jnp.broadcast_to(m_new, m_ref.shape[1:])


def _mlstm_post_body(hf_ref, hb_ref, og_ref, ng_ref, w_ref, g_ref, h_ref, o_ref):
    H, dv = MLSTM_HEADS, MLSTM_V
    hs = hf_ref[0, 0] + hb_ref[0, 0]
    parts = []
    for hd in range(H):
        x = hs[:, hd * dv:(hd + 1) * dv]
        xc = x - jnp.mean(x, axis=1, keepdims=True)
        parts.append(xc * lax.rsqrt(jnp.mean(xc * xc, axis=1, keepdims=True) + EPS))
    y = jnp.concatenate(parts, axis=1) * ng_ref[...] * _sigmoid(og_ref[0].astype(F32))
    o_ref[0] = h_ref[0] + _rms(_dot(y.astype(BF16), w_ref[...]), g_ref[...])


def _mlstm_mixer(h, g0, w_in, b_gates, norm_g, w_out, g1, tm):
    B, S, D = h.shape
    H, dk, dv = MLSTM_HEADS, MLSTM_QK, MLSTM_V
    o1, o2, o3, o4 = H * dk, 2 * H * dk, 2 * H * dk + H * dv, 2 * H * dk + 2 * H * dv

    def pad_heads(w):
        w = w.reshape(D, H, dk)
        return jnp.concatenate([w, jnp.zeros((D, H, 128 - dk), w.dtype)], -1).reshape(D, H * 128)

    wq = pad_heads(w_in[:, :o1]).astype(BF16)
    wkT = pad_heads(w_in[:, o1:o2]).T.astype(BF16)
    wv = w_in[:, o2:o3].astype(BF16)
    wog = w_in[:, o3:o4].astype(BF16)
    wg = w_in[:, o4:].astype(BF16)
    q, v, og, g, kT, gT = _norm_proj(
        h, g0,
        [(wq, None, 1.0, BF16), (wv, None, 1.0, BF16), (wog, None, 1.0, BF16), (wg, b_gates, 1.0, F32)],
        [(wkT, None, dk ** -0.5, BF16), (wg.T, b_gates, 1.0, F32)], tm)
    g = jnp.moveaxis(g.reshape(B, S, 2, 2 * H), 2, 0)
    gT = jnp.moveaxis(gT.reshape(B, 2, 2 * H, S), 1, 0)
    NC = S // CHUNK

    def ck(dr, ci):
        return ci + dr * (NC - 1 - 2 * ci)

    hout = pl.pallas_call(
        _mlstm_body, grid=(2, B, NC),
        in_specs=[pl.BlockSpec((1, CHUNK, H * 128), lambda dr, b, ci: (b, ck(dr, ci), 0)),
                  pl.BlockSpec((1, H * 128, CHUNK), lambda dr, b, ci: (b, 0, ck(dr, ci))),
                  pl.BlockSpec((1, CHUNK, H * dv), lambda dr, b, ci: (b, ck(dr, ci), 0)),
                  pl.BlockSpec((1, 1, CHUNK, 2 * H), lambda dr, b, ci: (dr, b, ck(dr, ci), 0)),
                  pl.BlockSpec((1, 1, 2 * H, CHUNK), lambda dr, b, ci: (dr, b, 0, ck(dr, ci)))],
        out_specs=pl.BlockSpec((1, 1, CHUNK, H * dv), lambda dr, b, ci: (dr, b, ck(dr, ci), 0)),
        out_shape=jax.ShapeDtypeStruct((2, B, S, H * dv), F32),
        scratch_shapes=[pltpu.VMEM((H, 128, 2 * dv), F32), pltpu.VMEM((H, 1, 128), F32)],
        compiler_params=_params("parallel", "parallel", "arbitrary"), name="mlstm_scan",
    )(q, kT, v, g, gT)

    return pl.pallas_call(
        _mlstm_post_body, grid=(B, S // tm),
        in_specs=[pl.BlockSpec((1, 1, tm, H * dv), lambda b, i: (0, b, i, 0)),
                  pl.BlockSpec((1, 1, tm, H * dv), lambda b, i: (1, b, i, 0)),
                  pl.BlockSpec((1, tm, H * dv), lambda b, i: (b, i, 0)),
                  _resident((1, H * dv)), _resident((H * dv, D)), _resident((1, D)),
                  pl.BlockSpec((1, tm, D), lambda b, i: (b, i, 0))],
        out_specs=pl.BlockSpec((1, tm, D), lambda b, i: (b, i, 0)),
        out_shape=jax.ShapeDtypeStruct((B, S, D), F32),
        compiler_params=_params("parallel", "parallel"), name="mlstm_post",
    )(hout, hout, og, norm_g.reshape(1, -1), w_out.astype(BF16), g1.reshape(1, D), h)


def _ret_body(q_ref, kT_ref, v_ref, dl_ref, o_ref, r_ref):
    H, L, dk, dv = RET_HEADS, CHUNK, RET_QK, RET_V
    dr = pl.program_id(0)

    @pl.when(pl.program_id(2) == 0)
    def _():
        r_ref[...] = jnp.zeros_like(r_ref)

    r = lax.broadcasted_iota(jnp.int32, (L, L), 0)
    c = lax.broadcasted_iota(jnp.int32, (L, L), 1)
    rel = (r - c) * (1 - 2 * dr)
    relf = jnp.maximum(rel, 0).astype(F32)
    t_col = lax.broadcasted_iota(jnp.int32, (L, 1), 0)
    s_row = lax.broadcasted_iota(jnp.int32, (1, L), 1)
    xi_pow = (t_col + 1 + dr * (L - 1 - 2 * t_col)).astype(F32)
    zeta_pow = (L - 1 - s_row + dr * (2 * s_row - (L - 1))).astype(F32)
    lg = _log_sigmoid(dl_ref[0])
    for hd in range(H):
        lgh = lg[:, hd:hd + 1]
        dmask = jnp.where(rel >= 0, jnp.exp(relf * lgh), 0.0)
        qc = q_ref[0, :, hd * dk:(hd + 1) * dk]
        kT = kT_ref[0, hd * dk:(hd + 1) * dk, :]
        vc = v_ref[0, :, hd * dv:(hd + 1) * dv]
        state = r_ref[hd]
        inner = _dot((_dot(qc, kT) * dmask).astype(BF16), vc)
        cross = _dot(qc, state.astype(BF16)) * jnp.exp(xi_pow * lgh)
        o_ref[0, 0, :, hd * dv:(hd + 1) * dv] = inner + cross
        kz = (kT * jnp.exp(zeta_pow * lgh)).astype(BF16)
        r_ref[hd] = jnp.exp(L * lgh) * state + _dot(kz, vc)


def _ret_post_body(yf_ref, yb_ref, gate_ref, ng_ref, w_ref, g_ref, h_ref, o_ref):
    H, dv = RET_HEADS, RET_V
    ys = yf_ref[0, 0] + yb_ref[0, 0]
    parts = []
    for hd in range(H):
        x = ys[:, hd * dv:(hd + 1) * dv]
        xc = x - jnp.mean(x, axis=1, keepdims=True)
        parts.append(xc * lax.rsqrt(jnp.mean(xc * xc, axis=1, keepdims=True) + EPS))
    gate = gate_ref[0].astype(F32)
    y = jnp.concatenate(parts, axis=1) * ng_ref[...] * (gate * _sigmoid(gate))
    o_ref[0] = h_ref[0] + _rms(_dot(y.astype(BF16), w_ref[...]), g_ref[...])


def _ret_mixer(h, g0, w_in, decay_logit, norm_g, w_o, g1, tm):
    B, S, D = h.shape
    H, dk, dv = RET_HEADS, RET_QK, RET_V
    o1, o2, o3 = H * dk, 2 * H * dk, 2 * H * dk + H * dv
    wq = w_in[:, :o1].astype(BF16)
    wkT = w_in[:, o1:o2].T.astype(BF16)
    wv = w_in[:, o2:o3].astype(BF16)
    wgate = w_in[:, o3:].astype(BF16)
    q, v, gate, kT = _norm_proj(
        h, g0, [(wq, None, 1.0, BF16), (wv, None, 1.0, BF16), (wgate, None, 1.0, BF16)],
        [(wkT, None, dk ** -0.5, BF16)], tm)
    NC = S // CHUNK

    def ck(dr, ci):
        return ci + dr * (NC - 1 - 2 * ci)

    y = pl.pallas_call(
        _ret_body, grid=(2, B, NC),
        in_specs=[pl.BlockSpec((1, CHUNK, H * dk), lambda dr, b, ci: (b, ck(dr, ci), 0)),
                  pl.BlockSpec((1, H * dk, CHUNK), lambda dr, b, ci: (b, 0, ck(dr, ci))),
                  pl.BlockSpec((1, CHUNK, H * dv), lambda dr, b, ci: (b, ck(dr, ci), 0)),
                  pl.BlockSpec((1, 1, H), lambda dr, b, ci: (dr, 0, 0))],
        out_specs=pl.BlockSpec((1, 1, CHUNK, H * dv), lambda dr, b, ci: (dr, b, ck(dr, ci), 0)),
        out_shape=jax.ShapeDtypeStruct((2, B, S, H * dv), F32),
        scratch_shapes=[pltpu.VMEM((H, dk, dv), F32)],
        compiler_params=_params("parallel", "parallel", "arbitrary"), name="ret_scan",
    )(q, kT, v, decay_logit.astype(F32).reshape(2, 1, H))

    return pl.pallas_call(
        _ret_post_body, grid=(B, S // tm),
        in_specs=[pl.BlockSpec((1, 1, tm, H * dv), lambda b, i: (0, b, i, 0)),
                  pl.BlockSpec((1, 1, tm, H * dv), lambda b, i: (1, b, i, 0)),
                  pl.BlockSpec((1, tm, H * dv), lambda b, i: (b, i, 0)),
                  _resident((1, H * dv)), _resident((H * dv, D)), _resident((1, D)),
                  pl.BlockSpec((1, tm, D), lambda b, i: (b, i, 0))],
        out_specs=pl.BlockSpec((1, tm, D), lambda b, i: (b, i, 0)),
        out_shape=jax.ShapeDtypeStruct((B, S, D), F32),
        compiler_params=_params("parallel", "parallel"), name="ret_post",
    )(y, y, gate, norm_g.reshape(1, -1), w_o.astype(BF16), g1.reshape(1, D), h)


def _tile(s, want):
    return min(want, s)


def kernel(x, norm_g, ffn_w_up, ffn_conv_w, ffn_conv_b, ffn_w_down, mla_w_dq, mla_q_norm_g, mla_w_uq, mla_w_dkv, mla_kv_norm_g, mla_w_ukv, mla_w_o, diff_w_qkv, diff_lambda, diff_subln_g, diff_w_o, mlstm_w_in, mlstm_b_gates, mlstm_norm_g, mlstm_w_out, ret_w_in, ret_decay_logit, ret_norm_g, ret_w_o):
    depth = norm_g.shape[0]
    S = x.shape[1]
    tm, tq, tk = _tile(S, 512), _tile(S, 256), _tile(S, 512)
    h = x
    for i in range(depth):
        kind, j = i % 4, i // 4
        g0, g1, g2, g3 = norm_g[i, 0], norm_g[i, 1], norm_g[i, 2], norm_g[i, 3]
        if kind == 0:
            h = _mla_mixer(h, g0, mla_w_dq[j], mla_q_norm_g[j], mla_w_uq[j], mla_w_dkv[j],
                           mla_kv_norm_g[j], mla_w_ukv[j], mla_w_o[j], g1, tm, tq, tk)
        elif kind == 1:
            h = _diff_mixer(h, g0, diff_w_qkv[j], diff_lambda[j], diff_subln_g[j], diff_w_o[j],
                            g1, i, tm, tq, tk)
        elif kind == 2:
            h = _mlstm_mixer(h, g0, mlstm_w_in[j], mlstm_b_gates[j], mlstm_norm_g[j],
                             mlstm_w_out[j], g1, tm)
        else:
            h = _ret_mixer(h, g0, ret_w_in[j], ret_decay_logit[j], ret_norm_g[j], ret_w_o[j], g1, tm)
        h = _conv_ffn(h, g2, ffn_w_up[i].astype(BF16), ffn_conv_w[i], ffn_conv_b[i],
                      ffn_w_down[i].astype(BF16), g3, tm)
    return h
```

```python
import functools
import math

import jax
import jax.numpy as jnp
from jax import lax
from jax.experimental import pallas as pl
from jax.experimental.pallas import tpu as pltpu

F32 = jnp.float32
BF16 = jnp.bfloat16

EPS = 1e-6
LOG2E = 1.4426950408889634
CHUNK = 128
HALO = 8

MLA_HEADS, MLA_NOPE, MLA_ROPE, MLA_V = 8, 128, 64, 128
MLA_Q_LORA, MLA_KV_LORA = 384, 256
ROPE_THETA = 10000.0
DIFF_HEADS, DIFF_HD = 8, 64
MLSTM_HEADS, MLSTM_QK, MLSTM_V = 8, 64, 128
RET_HEADS, RET_QK, RET_V = 4, 256, 512
CONV_W = 3

VMEM_LIMIT = 56 * 1024 * 1024


def _params(*sem):
    return pltpu.CompilerParams(dimension_semantics=sem, vmem_limit_bytes=VMEM_LIMIT)


def _resident(shape):
    nd = len(shape)
    return pl.BlockSpec(shape, lambda *_: (0,) * nd, pipeline_mode=pl.Buffered(1))


def _rms(x, g):
    return x * lax.rsqrt(jnp.mean(x * x, axis=-1, keepdims=True) + EPS) * g


def _dot(a, b):
    return jnp.dot(a, b, preferred_element_type=F32)


def _dot_nt(a, b):
    return lax.dot_general(a, b, (((1,), (1,)), ((), ())), preferred_element_type=F32)


def _log_sigmoid(x):
    return jnp.minimum(x, 0.0) - jnp.log1p(jnp.exp(-jnp.abs(x)))


def _sigmoid(x):
    return 1.0 / (1.0 + jnp.exp(-x))


COL_CHUNK = 512


def _proj_body(n_nat, n_t, nat_scales, t_scales, *refs):
    h_ref, g_ref = refs[0], refs[1]
    nat_w = refs[2:2 + n_nat]
    nat_b = refs[2 + n_nat:2 + 2 * n_nat]
    t_w = refs[2 + 2 * n_nat:2 + 2 * n_nat + n_t]
    t_b = refs[2 + 2 * n_nat + n_t:2 + 2 * n_nat + 2 * n_t]
    outs = refs[2 + 2 * n_nat + 2 * n_t:]
    nat_o, t_o = outs[:n_nat], outs[n_nat:]

    xn = _rms(h_ref[0], g_ref[...]).astype(BF16)
    for w_ref, b_ref, o_ref, sc in zip(nat_w, nat_b, nat_o, nat_scales):
        m = w_ref.shape[1]
        for c0 in range(0, m, COL_CHUNK):
            cw = min(COL_CHUNK, m - c0)
            y = _dot(xn, w_ref[:, c0:c0 + cw]) + b_ref[:, c0:c0 + cw]
            if sc != 1.0:
                y = y * sc
            o_ref[0, :, c0:c0 + cw] = y.astype(o_ref.dtype)
    for w_ref, b_ref, o_ref, sc in zip(t_w, t_b, t_o, t_scales):
        m = w_ref.shape[0]
        for r0 in range(0, m, COL_CHUNK):
            rw = min(COL_CHUNK, m - r0)
            y = _dot_nt(w_ref[r0:r0 + rw, :], xn) + b_ref[r0:r0 + rw, :]
            if sc != 1.0:
                y = y * sc
            o_ref[0, r0:r0 + rw, :] = y.astype(o_ref.dtype)


def _norm_proj(h, g, nat, tr, tm):
    B, S, D = h.shape
    args = [h, g.reshape(1, D)]
    specs = [pl.BlockSpec((1, tm, D), lambda b, i: (b, i, 0)), _resident((1, D))]
    for w, _, _, _ in nat:
        args.append(w)
        specs.append(_resident(w.shape))
    for w, bias, _, _ in nat:
        m = w.shape[1]
        args.append(jnp.zeros((1, m), F32) if bias is None else bias.reshape(1, m).astype(F32))
        specs.append(_resident((1, m)))
    for w, _, _, _ in tr:
        args.append(w)
        specs.append(_resident(w.shape))
    for w, bias, _, _ in tr:
        m = w.shape[0]
        args.append(jnp.zeros((m, 1), F32) if bias is None else bias.reshape(m, 1).astype(F32))
        specs.append(_resident((m, 1)))
    out_shape, out_specs = [], []
    for w, _, _, dt in nat:
        m = w.shape[1]
        out_shape.append(jax.ShapeDtypeStruct((B, S, m), dt))
        out_specs.append(pl.BlockSpec((1, tm, m), lambda b, i: (b, i, 0)))
    for w, _, _, dt in tr:
        m = w.shape[0]
        out_shape.append(jax.ShapeDtypeStruct((B, m, S), dt))
        out_specs.append(pl.BlockSpec((1, m, tm), lambda b, i: (b, 0, i)))
    body = functools.partial(_proj_body, len(nat), len(tr),
                             tuple(x[2] for x in nat), tuple(x[2] for x in tr))
    return pl.pallas_call(
        body, grid=(B, S // tm), in_specs=specs, out_specs=out_specs, out_shape=out_shape,
        compiler_params=_params("parallel", "parallel"), name="norm_proj")(*args)


def _out_body(a_ref, w_ref, g_ref, h_ref, o_ref):
    y = _dot(a_ref[0], w_ref[...])
    o_ref[0] = h_ref[0] + _rms(y, g_ref[...])


def _out_proj(a, w, g, h, tm):
    B, S, K = a.shape
    D = h.shape[-1]
    return pl.pallas_call(
        _out_body, grid=(B, S // tm),
        in_specs=[pl.BlockSpec((1, tm, K), lambda b, i: (b, i, 0)), _resident(w.shape),
                  _resident((1, D)), pl.BlockSpec((1, tm, D), lambda b, i: (b, i, 0))],
        out_specs=pl.BlockSpec((1, tm, D), lambda b, i: (b, i, 0)),
        out_shape=jax.ShapeDtypeStruct((B, S, D), F32),
        compiler_params=_params("parallel", "parallel"), name="out_proj")(a, w, g.reshape(1, D), h)


FFN_CHUNK = 256


def _ffn_body(tm, F, hp_ref, hm_ref, hn_ref, g2_ref, wup_ref, cw_ref, cb_ref, wdn_ref, g3_ref,
              o_ref, acc_ref, u_ref):
    i = pl.program_id(1)
    last = pl.num_programs(1) - 1
    g2 = g2_ref[...]
    hm = hm_ref[0]
    xp = _rms(hp_ref[0], g2) * jnp.where(i == 0, 0.0, 1.0)
    xn_ = _rms(hn_ref[0], g2) * jnp.where(i == last, 0.0, 1.0)
    x_ext = jnp.concatenate([xp, _rms(hm, g2), xn_], axis=0).astype(BF16)

    acc_ref[...] = jnp.zeros_like(acc_ref)

    n = F // FFN_CHUNK

    def up(c, slot):
        c0 = pl.multiple_of(c * FFN_CHUNK, FFN_CHUNK)
        for half, off in enumerate((0, F)):
            u_ref[slot, half] = _dot(x_ext, wup_ref[:, pl.ds(off + c0, FFN_CHUNK)])

    def down(c, slot):
        c0 = pl.multiple_of(c * FFN_CHUNK, FFN_CHUNK)
        halves = []
        for half, off in enumerate((0, F)):
            w = cw_ref[:, pl.ds(off + c0, FFN_CHUNK)]
            b = cb_ref[:, pl.ds(off + c0, FFN_CHUNK)]
            halves.append(b + w[0:1] * u_ref[slot, half, HALO - 1:HALO - 1 + tm, :]
                          + w[1:2] * u_ref[slot, half, HALO:HALO + tm, :]
                          + w[2:3] * u_ref[slot, half, HALO + 1:HALO + 1 + tm, :])
        hid = (jax.nn.gelu(halves[0]) * halves[1]).astype(BF16)
        acc_ref[...] += _dot(hid, wdn_ref[pl.ds(c0, FFN_CHUNK), :])

    def pair(p, carry):
        up(2 * p + 1, 1)
        down(2 * p, 0)
        up(2 * p + 2, 0)
        down(2 * p + 1, 1)
        return carry

    up(0, 0)
    lax.fori_loop(0, (n - 1) // 2, pair, 0)
    down(n - 1, 0)
    o_ref[0] = hm + _rms(acc_ref[...], g3_ref[...])


def _conv_ffn(h, g2, w_up, conv_w, conv_b, w_down, g3, tm):
    B, S, D = h.shape
    F = w_down.shape[0]
    assert F % FFN_CHUNK == 0 and (F // FFN_CHUNK) % 2 == 1
    nb = tm // HALO
    body = functools.partial(_ffn_body, tm, F)
    return pl.pallas_call(
        body, grid=(B, S // tm),
        in_specs=[
            pl.BlockSpec((1, HALO, D), lambda b, i: (b, jnp.maximum(i * nb - 1, 0), 0)),
            pl.BlockSpec((1, tm, D), lambda b, i: (b, i, 0)),
            pl.BlockSpec((1, HALO, D), lambda b, i: (b, jnp.minimum((i + 1) * nb, S // HALO - 1), 0)),
            _resident((1, D)), _resident(w_up.shape), _resident(conv_w.shape),
            _resident((1, 2 * F)), _resident(w_down.shape), _resident((1, D)),
        ],
        out_specs=pl.BlockSpec((1, tm, D), lambda b, i: (b, i, 0)),
        out_shape=jax.ShapeDtypeStruct((B, S, D), F32),
        scratch_shapes=[pltpu.VMEM((tm, D), F32),
                        pltpu.VMEM((2, 2, tm + 2 * HALO, FFN_CHUNK), F32)],
        compiler_params=_params("parallel", "parallel"), name="conv_ffn",
    )(h, h, h, g2.reshape(1, D), w_up, conv_w, conv_b.reshape(1, 2 * F), w_down, g3.reshape(1, D))


def _mla_proj_body(h_ref, g_ref, w1_ref, qg_ref, kvg_ref, wuqT_ref, wuk_ref, wuvT_ref,
                   tab_ref, cosT_ref, sinT_ref, qT_ref, k_ref, vT_ref):
    H = MLA_HEADS
    tm = h_ref.shape[1]
    a = _rms(h_ref[0], g_ref[...]).astype(BF16)
    t1 = _dot(a, w1_ref[...])
    cq = _rms(t1[:, :MLA_Q_LORA], qg_ref[...]).astype(BF16)
    ckv = _rms(t1[:, MLA_Q_LORA:MLA_Q_LORA + MLA_KV_LORA], kvg_ref[...]).astype(BF16)
    z = t1[:, MLA_Q_LORA + MLA_KV_LORA:] * tab_ref[...]
    lane = lax.broadcasted_iota(jnp.int32, z.shape, 1)
    kr = jnp.where(lane < MLA_ROPE, z + pltpu.roll(z, MLA_ROPE, 1), 0.0).astype(BF16)
    k_nope = _dot(ckv, wuk_ref[...]).astype(BF16)
    for hd in range(H):
        k_ref[0, hd] = jnp.concatenate([k_nope[:, hd * MLA_NOPE:(hd + 1) * MLA_NOPE], kr], axis=1)
    vT_ref[0] = _dot_nt(wuvT_ref[...], ckv).astype(BF16)
    scale = (MLA_NOPE + MLA_ROPE) ** -0.5 * LOG2E
    cosT, sinT = cosT_ref[...], sinT_ref[...]
    zeros = jnp.zeros((256 - MLA_NOPE - MLA_ROPE, tm), F32)
    for hd in range(H):
        blk = _dot_nt(wuqT_ref[hd * 256:(hd + 1) * 256, :], cq)
        r = blk[128:192] * cosT + blk[192:256] * sinT
        qT_ref[0, hd * 256:(hd + 1) * 256, :] = (
            jnp.concatenate([blk[:128], r, zeros], axis=0) * scale).astype(BF16)


QSUB = 256
MLA_TQ, MLA_TK, MLA_UNROLL = 512, 512, 5
DIFF_TQ, DIFF_TK, DIFF_UNROLL = 256, 256, 8


def _online_update(s, vc, m, l, acc):
    m_new = jnp.maximum(m, jnp.max(s, axis=0, keepdims=True))
    p = jnp.exp2(s - m_new)
    alpha = jnp.exp2(m - m_new)
    return (m_new, alpha * l + jnp.sum(p, axis=0, keepdims=True),
            alpha * acc + _dot(vc, p.astype(BF16)))


def _flash_init(dv):
    return (jnp.full((1, QSUB), -jnp.inf, F32), jnp.zeros((1, QSUB), F32), jnp.zeros((dv, QSUB), F32))


def _flash_body(tk, unroll, qT_ref, k_ref, vT_ref, o_ref):
    S = k_ref.shape[2]
    G = qT_ref.shape[2] // QSUB
    dv = vT_ref.shape[1]
    n = S // tk
    qs = [qT_ref[0, :, g * QSUB:(g + 1) * QSUB] for g in range(G)]

    def scores(j):
        kc = k_ref[0, 0, pl.ds(pl.multiple_of(j * tk, tk), tk), :]
        return tuple(_dot(kc, q) for q in qs)

    def consume(j, s, stats):
        vc = vT_ref[0, :, pl.ds(pl.multiple_of(j * tk, tk), tk)]
        return tuple(_online_update(s[g], vc, *stats[g]) for g in range(G))

    def step(j, carry):
        s, stats = carry
        s_next = scores(j + 1)
        return s_next, consume(j, s, stats)

    s, stats = lax.fori_loop(0, n - 1, step, (scores(0), tuple(_flash_init(dv) for _ in range(G))),
                             unroll=max(1, min(unroll, n - 1)))
    res = consume(n - 1, s, stats)
    for g, (_, l, acc) in enumerate(res):
        o_ref[0, g * QSUB:(g + 1) * QSUB, :] = (acc / l).T.astype(o_ref.dtype)


def _mla_mixer(h, g0, w_dq, qg, w_uq, w_dkv, kvg, w_ukv, w_o, g1, tm, tq, tk, unroll):
    B, S, D = h.shape
    H = MLA_HEADS
    half = MLA_ROPE // 2

    def rot_cols(w):
        return jnp.concatenate([-w[..., half:], w[..., :half]], axis=-1)

    wkr = w_dkv[:, MLA_KV_LORA:]
    w1 = jnp.concatenate([w_dq, w_dkv[:, :MLA_KV_LORA], wkr, rot_cols(wkr)], axis=1).astype(BF16)
    wq = w_uq.reshape(MLA_Q_LORA, H, MLA_NOPE + MLA_ROPE)
    wq_r = wq[..., MLA_NOPE:]
    wuqT = jnp.concatenate([wq[..., :MLA_NOPE], wq_r, rot_cols(wq_r)], axis=-1)
    wuqT = wuqT.reshape(MLA_Q_LORA, H * 256).T.astype(BF16)
    wkv = w_ukv.reshape(MLA_KV_LORA, H, MLA_NOPE + MLA_V)
    wuk = wkv[..., :MLA_NOPE].reshape(MLA_KV_LORA, H * MLA_NOPE).astype(BF16)
    wuvT = wkv[..., MLA_NOPE:].reshape(MLA_KV_LORA, H * MLA_V).T.astype(BF16)

    inv = ROPE_THETA ** (-jnp.arange(0, MLA_ROPE, 2, dtype=F32) / MLA_ROPE)
    ang = jnp.arange(S, dtype=F32)[:, None] * inv[None, :]
    cos, sin = jnp.cos(ang), jnp.sin(ang)
    cos2, sin2 = jnp.concatenate([cos, cos], 1), jnp.concatenate([sin, sin], 1)
    tab = jnp.concatenate([cos2, sin2], 1)

    qT, k, vT = pl.pallas_call(
        _mla_proj_body, grid=(B, S // tm),
        in_specs=[pl.BlockSpec((1, tm, D), lambda b, i: (b, i, 0)), _resident((1, D)),
                  _resident(w1.shape), _resident((1, MLA_Q_LORA)), _resident((1, MLA_KV_LORA)),
                  _resident(wuqT.shape), _resident(wuk.shape), _resident(wuvT.shape),
                  pl.BlockSpec((tm, 128), lambda b, i: (i, 0)),
                  pl.BlockSpec((MLA_ROPE, tm), lambda b, i: (0, i)),
                  pl.BlockSpec((MLA_ROPE, tm), lambda b, i: (0, i))],
        out_specs=[pl.BlockSpec((1, H * 256, tm), lambda b, i: (b, 0, i)),
                   pl.BlockSpec((1, H, tm, 256), lambda b, i: (b, 0, i, 0)),
                   pl.BlockSpec((1, H * MLA_V, tm), lambda b, i: (b, 0, i))],
        out_shape=[jax.ShapeDtypeStruct((B, H * 256, S), BF16),
                   jax.ShapeDtypeStruct((B, H, S, 256), BF16),
                   jax.ShapeDtypeStruct((B, H * MLA_V, S), BF16)],
        compiler_params=_params("parallel", "parallel"), name="mla_proj",
    )(h, g0.reshape(1, D), w1, qg.reshape(1, -1), kvg.reshape(1, -1), wuqT, wuk, wuvT,
      tab, cos2.T, sin2.T)

    o = pl.pallas_call(
        functools.partial(_flash_body, tk, unroll), grid=(B, H, S // tq),
        in_specs=[pl.BlockSpec((1, 256, tq), lambda b, hd, i: (b, hd, i)),
                  pl.BlockSpec((1, 1, S, 256), lambda b, hd, i: (b, hd, 0, 0)),
                  pl.BlockSpec((1, MLA_V, S), lambda b, hd, i: (b, hd, 0))],
        out_specs=pl.BlockSpec((1, tq, MLA_V), lambda b, hd, i: (b, i, hd)),
        out_shape=jax.ShapeDtypeStruct((B, S, H * MLA_V), BF16),
        compiler_params=_params("parallel", "parallel", "arbitrary"), name="mla_flash",
    )(qT, k, vT)
    return _out_proj(o, w_o.astype(BF16), g1, h, tm)


def _diff_flash_body(tk, unroll, lam_init, slope_ref, qT_ref, k_ref, vT_ref, lamp_ref, sg_ref, o_ref):
    S = k_ref.shape[1]
    tq = qT_ref.shape[2]
    G = tq // QSUB
    d = DIFF_HD
    hd = pl.program_id(1)
    q0 = pl.program_id(2) * tq
    slope = slope_ref[hd]
    row = lax.broadcasted_iota(jnp.int32, (2 * d, QSUB), 0)
    rel = (lax.broadcasted_iota(jnp.int32, (tk, QSUB), 1)
           - lax.broadcasted_iota(jnp.int32, (tk, QSUB), 0))
    qs, base = [], []
    for g in range(G):
        qT = qT_ref[0, :, g * QSUB:(g + 1) * QSUB]
        zero = jnp.zeros_like(qT)
        qs.append((jnp.where(row < d, qT, zero), jnp.where(row >= d, qT, zero)))
        base.append(slope * (rel + (q0 + g * QSUB)).astype(F32))

    n = S // tk

    def scores(j):
        kc = k_ref[0, pl.ds(pl.multiple_of(j * tk, tk), tk), :]
        return tuple(_dot(kc, q) for pair in qs for q in pair)

    def consume(j, s, stats):
        vc = vT_ref[0, :, pl.ds(pl.multiple_of(j * tk, tk), tk)]
        off = slope * (j * tk).astype(F32)
        out = []
        for g in range(G):
            bias = jnp.abs(base[g] - off)
            for c in (2 * g, 2 * g + 1):
                out.append(_online_update(s[c] - bias, vc, *stats[c]))
        return tuple(out)

    def step(j, carry):
        s, stats = carry
        s_next = scores(j + 1)
        return s_next, consume(j, s, stats)

    s, stats = lax.fori_loop(
        0, n - 1, step, (scores(0), tuple(_flash_init(2 * d) for _ in range(2 * G))),
        unroll=max(1, min(unroll, n - 1)))
    res = consume(jnp.int32(n - 1), s, stats)
    lp = lamp_ref[...]
    lam = (jnp.exp(jnp.sum(lp[0:1] * lp[1:2], axis=1, keepdims=True))
           - jnp.exp(jnp.sum(lp[2:3] * lp[3:4], axis=1, keepdims=True)) + lam_init)
    gain = sg_ref[...] * (1.0 - lam_init)
    for g in range(G):
        (_, l1, a1), (_, l2, a2) = res[2 * g], res[2 * g + 1]
        oT = a1 / l1 - lam * (a2 / l2)
        oT = oT * lax.rsqrt(jnp.mean(oT * oT, axis=0, keepdims=True) + EPS) * gain
        o_ref[0, g * QSUB:(g + 1) * QSUB, :] = oT.T.astype(o_ref.dtype)


def _diff_mixer(h, g0, w_qkv, lam_p, subln_g, w_o, g1, layer_idx, tm, tq, tk, unroll):
    B, S, D = h.shape
    H, d = DIFF_HEADS, DIFF_HD
    scale = d ** -0.5 * LOG2E
    wqT = w_qkv[:, :D].T.astype(BF16)
    wk = w_qkv[:, D:2 * D].astype(BF16)
    wvT = w_qkv[:, 2 * D:].T.astype(BF16)
    k, qT, vT = _norm_proj(h, g0, [(wk, None, 1.0, BF16)],
                           [(wqT, None, scale, BF16), (wvT, None, 1.0, BF16)], tm)
    lam_init = 0.8 - 0.6 * math.exp(-0.3 * layer_idx)
    slopes = (2.0 ** (-8.0 * jnp.arange(1, H + 1, dtype=F32) / H)) * LOG2E
    o = pl.pallas_call(
        functools.partial(_diff_flash_body, tk, unroll, lam_init), grid=(B, H, S // tq),
        in_specs=[pl.BlockSpec(memory_space=pltpu.SMEM),
                  pl.BlockSpec((1, 2 * d, tq), lambda b, hd, i: (b, hd, i)),
                  pl.BlockSpec((1, S, 2 * d), lambda b, hd, i: (b, 0, hd)),
                  pl.BlockSpec((1, 2 * d, S), lambda b, hd, i: (b, hd, 0)),
                  pl.BlockSpec((4, d), lambda b, hd, i: (0, 0)),
                  pl.BlockSpec((2 * d, 1), lambda b, hd, i: (0, 0))],
        out_specs=pl.BlockSpec((1, tq, 2 * d), lambda b, hd, i: (b, i, hd)),
        out_shape=jax.ShapeDtypeStruct((B, S, H * 2 * d), BF16),
        compiler_params=_params("parallel", "parallel", "arbitrary"), name="diff_flash",
    )(slopes, qT, k, vT, lam_p.astype(F32), subln_g.reshape(2 * d, 1).astype(F32))
    return _out_proj(o, w_o.astype(BF16), g1, h, tm)


def _mlstm_body(q_ref, kT_ref, v_ref, g_ref, gT_ref, o_ref, c_ref, m_ref):
    H, L, dv = MLSTM_HEADS, CHUNK, MLSTM_V
    sgn = 1 - 2 * pl.program_id(0)

    @pl.when(pl.program_id(2) == 0)
    def _():
        c_ref[...] = jnp.zeros_like(c_ref)
        m_ref[...] = jnp.zeros_like(m_ref)

    r = lax.broadcasted_iota(jnp.int32, (L, L), 0)
    c = lax.broadcasted_iota(jnp.int32, (L, L), 1)
    causal = (r - c) * sgn >= 0
    causal_t = (c - r) * sgn >= 0
    lane = lax.broadcasted_iota(jnp.int32, (L, dv), 1)
    ones_blk = jnp.where(lane == 0, 1.0, 0.0).astype(BF16)
    g = g_ref[0, 0]
    gT = gT_ref[0, 0]
    for hd in range(H):
        i_row = gT[hd:hd + 1, :]
        lf_row = _log_sigmoid(gT[H + hd:H + hd + 1, :])
        lf_col = _log_sigmoid(g[:, H + hd:H + hd + 1])
        bcum_col = jnp.sum(jnp.where(causal, lf_row, 0.0), axis=1, keepdims=True)
        bcum_row = jnp.sum(jnp.where(causal_t, lf_col, 0.0), axis=0, keepdims=True)
        b_last = jnp.sum(lf_row, axis=1, keepdims=True)
        m_prev = m_ref[hd][:, 0:1]

        dmat = jnp.where(causal, bcum_col - bcum_row + i_row, -jnp.inf)
        m_inter = bcum_col + m_prev
        m_t = jnp.maximum(jnp.max(dmat, axis=1, keepdims=True), m_inter)
        w_intra = jnp.exp(dmat - m_t)
        w_inter = jnp.exp(m_inter - m_t)

        qc = q_ref[0, :, hd * 128:(hd + 1) * 128]
        kT = kT_ref[0, hd * 128:(hd + 1) * 128, :]
        v_ext = jnp.concatenate([v_ref[0, :, hd * dv:(hd + 1) * dv], ones_blk], axis=1)
        c_ext = c_ref[hd]
        sqk = (_dot(qc, kT) * w_intra).astype(BF16)
        nd = w_inter * _dot(qc, c_ext.astype(BF16)) + _dot(sqk, v_ext)
        den = nd[:, dv:dv + 1]
        o_ref[0, 0, :, hd * dv:(hd + 1) * dv] = (
            nd[:, :dv] / jnp.maximum(jnp.abs(den), jnp.exp(-m_t)))

        g_s = b_last - bcum_row + i_row
        m_new = jnp.maximum(b_last + m_prev, jnp.max(g_s, axis=1, keepdims=True))
        w_s = jnp.exp(g_s - m_new)
        decay = jnp.exp(b_last + m_prev - m_new)
        c_ref[hd] = decay * c_ext + _dot((kT * w_s).astype(BF16), v_ext)
        m_ref[hd] = jnp.broadcast_to(m_new, m_ref.shape[1:])


def _mlstm_post_body(hf_ref, hb_ref, og_ref, ng_ref, w_ref, g_ref, h_ref, o_ref):
    H, dv = MLSTM_HEADS, MLSTM_V
    hs = hf_ref[0, 0] + hb_ref[0, 0]
    parts = []
    for hd in range(H):
        x = hs[:, hd * dv:(hd + 1) * dv]
        xc = x - jnp.mean(x, axis=1, keepdims=True)
        parts.append(xc * lax.rsqrt(jnp.mean(xc * xc, axis=1, keepdims=True) + EPS))
    y = jnp.concatenate(parts, axis=1) * ng_ref[...] * _sigmoid(og_ref[0].astype(F32))
    o_ref[0] = h_ref[0] + _rms(_dot(y.astype(BF16), w_ref[...]), g_ref[...])


def _mlstm_mixer(h, g0, w_in, b_gates, norm_g, w_out, g1, tm):
    B, S, D = h.shape
    H, dk, dv = MLSTM_HEADS, MLSTM_QK, MLSTM_V
    o1, o2, o3, o4 = H * dk, 2 * H * dk, 2 * H * dk + H * dv, 2 * H * dk + 2 * H * dv

    def pad_heads(w):
        w = w.reshape(D, H, dk)
        return jnp.concatenate([w, jnp.zeros((D, H, 128 - dk), w.dtype)], -1).reshape(D, H * 128)

    wq = pad_heads(w_in[:, :o1]).astype(BF16)
    wkT = pad_heads(w_in[:, o1:o2]).T.astype(BF16)
    wv = w_in[:, o2:o3].astype(BF16)
    wog = w_in[:, o3:o4].astype(BF16)
    wg = w_in[:, o4:].astype(BF16)
    q, v, og, g, kT, gT = _norm_proj(
        h, g0,
        [(wq, None, 1.0, BF16), (wv, None, 1.0, BF16), (wog, None, 1.0, BF16), (wg, b_gates, 1.0, F32)],
        [(wkT, None, dk ** -0.5, BF16), (wg.T, b_gates, 1.0, F32)], tm)
    g = jnp.moveaxis(g.reshape(B, S, 2, 2 * H), 2, 0)
    gT = jnp.moveaxis(gT.reshape(B, 2, 2 * H, S), 1, 0)
    NC = S // CHUNK

    def ck(dr, ci):
        return ci + dr * (NC - 1 - 2 * ci)

    hout = pl.pallas_call(
        _mlstm_body, grid=(2, B, NC),
        in_specs=[pl.BlockSpec((1, CHUNK, H * 128), lambda dr, b, ci: (b, ck(dr, ci), 0)),
                  pl.BlockSpec((1, H * 128, CHUNK), lambda dr, b, ci: (b, 0, ck(dr, ci))),
                  pl.BlockSpec((1, CHUNK, H * dv), lambda dr, b, ci: (b, ck(dr, ci), 0)),
                  pl.BlockSpec((1, 1, CHUNK, 2 * H), lambda dr, b, ci: (dr, b, ck(dr, ci), 0)),
                  pl.BlockSpec((1, 1, 2 * H, CHUNK), lambda dr, b, ci: (dr, b, 0, ck(dr, ci)))],
        out_specs=pl.BlockSpec((1, 1, CHUNK, H * dv), lambda dr, b, ci: (dr, b, ck(dr, ci), 0)),
        out_shape=jax.ShapeDtypeStruct((2, B, S, H * dv), F32),
        scratch_shapes=[pltpu.VMEM((H, 128, 2 * dv), F32), pltpu.VMEM((H, 1, 128), F32)],
        compiler_params=_params("parallel", "parallel", "arbitrary"), name="mlstm_scan",
    )(q, kT, v, g, gT)

    return pl.pallas_call(
        _mlstm_post_body, grid=(B, S // tm),
        in_specs=[pl.BlockSpec((1, 1, tm, H * dv), lambda b, i: (0, b, i, 0)),
                  pl.BlockSpec((1, 1, tm, H * dv), lambda b, i: (1, b, i, 0)),
                  pl.BlockSpec((1, tm, H * dv), lambda b, i: (b, i, 0)),
                  _resident((1, H * dv)), _resident((H * dv, D)), _resident((1, D)),
                  pl.BlockSpec((1, tm, D), lambda b, i: (b, i, 0))],
        out_specs=pl.BlockSpec((1, tm, D), lambda b, i: (b, i, 0)),
        out_shape=jax.ShapeDtypeStruct((B, S, D), F32),
        compiler_params=_params("parallel", "parallel"), name="mlstm_post",
    )(hout, hout, og, norm_g.reshape(1, -1), w_out.astype(BF16), g1.reshape(1, D), h)


def _ret_body(q_ref, kT_ref, v_ref, dl_ref, o_ref, r_ref):
    H, L, dk, dv = RET_HEADS, CHUNK, RET_QK, RET_V
    dr = pl.program_id(0)

    @pl.when(pl.program_id(2) == 0)
    def _():
        r_ref[...] = jnp.zeros_like(r_ref)

    r = lax.broadcasted_iota(jnp.int32, (L, L), 0)
    c = lax.broadcasted_iota(jnp.int32, (L, L), 1)
    rel = (r - c) * (1 - 2 * dr)
    relf = jnp.maximum(rel, 0).astype(F32)
    t_col = lax.broadcasted_iota(jnp.int32, (L, 1), 0)
    s_row = lax.broadcasted_iota(jnp.int32, (1, L), 1)
    xi_pow = (t_col + 1 + dr * (L - 1 - 2 * t_col)).astype(F32)
    zeta_pow = (L - 1 - s_row + dr * (2 * s_row - (L - 1))).astype(F32)
    lg = _log_sigmoid(dl_ref[0])
    for hd in range(H):
        lgh = lg[:, hd:hd + 1]
        dmask = jnp.where(rel >= 0, jnp.exp(relf * lgh), 0.0)
        qc = q_ref[0, :, hd * dk:(hd + 1) * dk]
        kT = kT_ref[0, hd * dk:(hd + 1) * dk, :]
        vc = v_ref[0, :, hd * dv:(hd + 1) * dv]
        state = r_ref[hd]
        inner = _dot((_dot(qc, kT) * dmask).astype(BF16), vc)
        cross = _dot(qc, state.astype(BF16)) * jnp.exp(xi_pow * lgh)
        o_ref[0, 0, :, hd * dv:(hd + 1) * dv] = inner + cross
        kz = (kT * jnp.exp(zeta_pow * lgh)).astype(BF16)
        r_ref[hd] = jnp.exp(L * lgh) * state + _dot(kz, vc)


def _ret_post_body(yf_ref, yb_ref, gate_ref, ng_ref, w_ref, g_ref, h_ref, o_ref):
    H, dv = RET_HEADS, RET_V
    ys = yf_ref[0, 0] + yb_ref[0, 0]
    parts = []
    for hd in range(H):
        x = ys[:, hd * dv:(hd + 1) * dv]
        xc = x - jnp.mean(x, axis=1, keepdims=True)
        parts.append(xc * lax.rsqrt(jnp.mean(xc * xc, axis=1, keepdims=True) + EPS))
    gate = gate_ref[0].astype(F32)
    y = jnp.concatenate(parts, axis=1) * ng_ref[...] * (gate * _sigmoid(gate))
    o_ref[0] = h_ref[0] + _rms(_dot(y.astype(BF16), w_ref[...]), g_ref[...])


def _ret_mixer(h, g0, w_in, decay_logit, norm_g, w_o, g1, tm):
    B, S, D = h.shape
    H, dk, dv = RET_HEADS, RET_QK, RET_V
    o1, o2, o3 = H * dk, 2 * H * dk, 2 * H * dk + H * dv
    wq = w_in[:, :o1].astype(BF16)
    wkT = w_in[:, o1:o2].T.astype(BF16)
    wv = w_in[:, o2:o3].astype(BF16)
    wgate = w_in[:, o3:].astype(BF16)
    q, v, gate, kT = _norm_proj(
        h, g0, [(wq, None, 1.0, BF16), (wv, None, 1.0, BF16), (wgate, None, 1.0, BF16)],
        [(wkT, None, dk ** -0.5, BF16)], tm)
    NC = S // CHUNK

    def ck(dr, ci):
        return ci + dr * (NC - 1 - 2 * ci)

    y = pl.pallas_call(
        _ret_body, grid=(2, B, NC),
        in_specs=[pl.BlockSpec((1, CHUNK, H * dk), lambda dr, b, ci: (b, ck(dr, ci), 0)),
                  pl.BlockSpec((1, H * dk, CHUNK), lambda dr, b, ci: (b, 0, ck(dr, ci))),
                  pl.BlockSpec((1, CHUNK, H * dv), lambda dr, b, ci: (b, ck(dr, ci), 0)),
                  pl.BlockSpec((1, 1, H), lambda dr, b, ci: (dr, 0, 0))],
        out_specs=pl.BlockSpec((1, 1, CHUNK, H * dv), lambda dr, b, ci: (dr, b, ck(dr, ci), 0)),
        out_shape=jax.ShapeDtypeStruct((2, B, S, H * dv), F32),
        scratch_shapes=[pltpu.VMEM((H, dk, dv), F32)],
        compiler_params=_params("parallel", "parallel", "arbitrary"), name="ret_scan",
    )(q, kT, v, decay_logit.astype(F32).reshape(2, 1, H))

    return pl.pallas_call(
        _ret_post_body, grid=(B, S // tm),
        in_specs=[pl.BlockSpec((1, 1, tm, H * dv), lambda b, i: (0, b, i, 0)),
                  pl.BlockSpec((1, 1, tm, H * dv), lambda b, i: (1, b, i, 0)),
                  pl.BlockSpec((1, tm, H * dv), lambda b, i: (b, i, 0)),
                  _resident((1, H * dv)), _resident((H * dv, D)), _resident((1, D)),
                  pl.BlockSpec((1, tm, D), lambda b, i: (b, i, 0))],
        out_specs=pl.BlockSpec((1, tm, D), lambda b, i: (b, i, 0)),
        out_shape=jax.ShapeDtypeStruct((B, S, D), F32),
        compiler_params=_params("parallel", "parallel"), name="ret_post",
    )(y, y, gate, norm_g.reshape(1, -1), w_o.astype(BF16), g1.reshape(1, D), h)


def _tile(s, want):
    return min(want, s)


def kernel(x, norm_g, ffn_w_up, ffn_conv_w, ffn_conv_b, ffn_w_down, mla_w_dq, mla_q_norm_g, mla_w_uq, mla_w_dkv, mla_kv_norm_g, mla_w_ukv, mla_w_o, diff_w_qkv, diff_lambda, diff_subln_g, diff_w_o, mlstm_w_in, mlstm_b_gates, mlstm_norm_g, mlstm_w_out, ret_w_in, ret_decay_logit, ret_norm_g, ret_w_o):
    depth = norm_g.shape[0]
    S = x.shape[1]
    tm = _tile(S, 512)
    mla_t = (_tile(S, MLA_TQ), _tile(S, MLA_TK), MLA_UNROLL)
    diff_t = (_tile(S, DIFF_TQ), _tile(S, DIFF_TK), DIFF_UNROLL)
    h = x
    for i in range(depth):
        kind, j = i % 4, i // 4
        g0, g1, g2, g3 = norm_g[i, 0], norm_g[i, 1], norm_g[i, 2], norm_g[i, 3]
        if kind == 0:
            h = _mla_mixer(h, g0, mla_w_dq[j], mla_q_norm_g[j], mla_w_uq[j], mla_w_dkv[j],
                           mla_kv_norm_g[j], mla_w_ukv[j], mla_w_o[j], g1, tm, *mla_t)
        elif kind == 1:
            h = _diff_mixer(h, g0, diff_w_qkv[j], diff_lambda[j], diff_subln_g[j], diff_w_o[j],
                            g1, i, tm, *diff_t)
        elif kind == 2:
            h = _mlstm_mixer(h, g0, mlstm_w_in[j], mlstm_b_gates[j], mlstm_norm_g[j],
                             mlstm_w_out[j], g1, tm)
        else:
            h = _ret_mixer(h, g0, ret_w_in[j], ret_decay_logit[j], ret_norm_g[j], ret_w_o[j], g1, tm)
        h = _conv_ffn(h, g2, ffn_w_up[i].astype(BF16), ffn_conv_w[i], ffn_conv_b[i],
                      ffn_w_down[i].astype(BF16), g3, tm)
    return h
```

```python
import functools
import math

import jax
import jax.numpy as jnp
import numpy as np
from jax import lax
from jax.experimental import pallas as pl
from jax.experimental.pallas import tpu as pltpu

F32 = jnp.float32
BF16 = jnp.bfloat16

EPS = 1e-6
LOG2E = 1.4426950408889634
CHUNK = 128
HALO = 8

MLA_HEADS, MLA_NOPE, MLA_ROPE, MLA_V = 8, 128, 64, 128
MLA_Q_LORA, MLA_KV_LORA = 384, 256
ROPE_THETA = 10000.0
DIFF_HEADS, DIFF_HD = 8, 64
MLSTM_HEADS, MLSTM_QK, MLSTM_V = 8, 64, 128
RET_HEADS, RET_QK, RET_V = 4, 256, 512
CONV_W = 3

VMEM_LIMIT = 56 * 1024 * 1024


def _params(*sem):
    return pltpu.CompilerParams(dimension_semantics=sem, vmem_limit_bytes=VMEM_LIMIT)


def _resident(shape):
    nd = len(shape)
    return pl.BlockSpec(shape, lambda *_: (0,) * nd, pipeline_mode=pl.Buffered(1))


def _rms(x, g):
    return x * lax.rsqrt(jnp.mean(x * x, axis=-1, keepdims=True) + EPS) * g


def _dot(a, b):
    return jnp.dot(a, b, preferred_element_type=F32)


def _dot_nt(a, b):
    return lax.dot_general(a, b, (((1,), (1,)), ((), ())), preferred_element_type=F32)


def _log_sigmoid(x):
    return jnp.minimum(x, 0.0) - jnp.log1p(jnp.exp(-jnp.abs(x)))


def _sigmoid(x):
    return 1.0 / (1.0 + jnp.exp(-x))


COL_CHUNK = 512


def _proj_body(n_nat, n_t, nat_scales, t_scales, *refs):
    h_ref, g_ref = refs[0], refs[1]
    nat_w = refs[2:2 + n_nat]
    nat_b = refs[2 + n_nat:2 + 2 * n_nat]
    t_w = refs[2 + 2 * n_nat:2 + 2 * n_nat + n_t]
    t_b = refs[2 + 2 * n_nat + n_t:2 + 2 * n_nat + 2 * n_t]
    outs = refs[2 + 2 * n_nat + 2 * n_t:]
    nat_o, t_o = outs[:n_nat], outs[n_nat:]

    xn = _rms(h_ref[0], g_ref[...]).astype(BF16)
    for w_ref, b_ref, o_ref, sc in zip(nat_w, nat_b, nat_o, nat_scales):
        m = w_ref.shape[1]
        for c0 in range(0, m, COL_CHUNK):
            cw = min(COL_CHUNK, m - c0)
            y = _dot(xn, w_ref[:, c0:c0 + cw]) + b_ref[:, c0:c0 + cw]
            if sc != 1.0:
                y = y * sc
            o_ref[0, :, c0:c0 + cw] = y.astype(o_ref.dtype)
    for w_ref, b_ref, o_ref, sc in zip(t_w, t_b, t_o, t_scales):
        m = w_ref.shape[0]
        for r0 in range(0, m, COL_CHUNK):
            rw = min(COL_CHUNK, m - r0)
            y = _dot_nt(w_ref[r0:r0 + rw, :], xn) + b_ref[r0:r0 + rw, :]
            if sc != 1.0:
                y = y * sc
            o_ref[0, r0:r0 + rw, :] = y.astype(o_ref.dtype)


def _norm_proj(h, g, nat, tr, tm):
    B, S, D = h.shape
    args = [h, g.reshape(1, D)]
    specs = [pl.BlockSpec((1, tm, D), lambda b, i: (b, i, 0)), _resident((1, D))]
    for w, _, _, _ in nat:
        args.append(w)
        specs.append(_resident(w.shape))
    for w, bias, _, _ in nat:
        m = w.shape[1]
        args.append(jnp.zeros((1, m), F32) if bias is None else bias.reshape(1, m).astype(F32))
        specs.append(_resident((1, m)))
    for w, _, _, _ in tr:
        args.append(w)
        specs.append(_resident(w.shape))
    for w, bias, _, _ in tr:
        m = w.shape[0]
        args.append(jnp.zeros((m, 1), F32) if bias is None else bias.reshape(m, 1).astype(F32))
        specs.append(_resident((m, 1)))
    out_shape, out_specs = [], []
    for w, _, _, dt in nat:
        m = w.shape[1]
        out_shape.append(jax.ShapeDtypeStruct((B, S, m), dt))
        out_specs.append(pl.BlockSpec((1, tm, m), lambda b, i: (b, i, 0)))
    for w, _, _, dt in tr:
        m = w.shape[0]
        out_shape.append(jax.ShapeDtypeStruct((B, m, S), dt))
        out_specs.append(pl.BlockSpec((1, m, tm), lambda b, i: (b, 0, i)))
    body = functools.partial(_proj_body, len(nat), len(tr),
                             tuple(x[2] for x in nat), tuple(x[2] for x in tr))
    return pl.pallas_call(
        body, grid=(B, S // tm), in_specs=specs, out_specs=out_specs, out_shape=out_shape,
        compiler_params=_params("parallel", "parallel"), name="norm_proj")(*args)


def _out_body(a_ref, w_ref, g_ref, h_ref, o_ref):
    y = _dot(a_ref[0], w_ref[...])
    o_ref[0] = h_ref[0] + _rms(y, g_ref[...])


def _out_proj(a, w, g, h, tm):
    B, S, K = a.shape
    D = h.shape[-1]
    return pl.pallas_call(
        _out_body, grid=(B, S // tm),
        in_specs=[pl.BlockSpec((1, tm, K), lambda b, i: (b, i, 0)), _resident(w.shape),
                  _resident((1, D)), pl.BlockSpec((1, tm, D), lambda b, i: (b, i, 0))],
        out_specs=pl.BlockSpec((1, tm, D), lambda b, i: (b, i, 0)),
        out_shape=jax.ShapeDtypeStruct((B, S, D), F32),
        compiler_params=_params("parallel", "parallel"), name="out_proj")(a, w, g.reshape(1, D), h)


FFN_CHUNK = 256


def _ffn_body(tm, F, hp_ref, hm_ref, hn_ref, g2_ref, wup_ref, cw_ref, cb_ref, wdn_ref, g3_ref,
              o_ref, acc_ref, u_ref):
    i = pl.program_id(1)
    last = pl.num_programs(1) - 1
    g2 = g2_ref[...]
    hm = hm_ref[0]
    xp = _rms(hp_ref[0], g2) * jnp.where(i == 0, 0.0, 1.0)
    xn_ = _rms(hn_ref[0], g2) * jnp.where(i == last, 0.0, 1.0)
    x_ext = jnp.concatenate([xp, _rms(hm, g2), xn_], axis=0).astype(BF16)

    acc_ref[...] = jnp.zeros_like(acc_ref)

    n = F // FFN_CHUNK

    def up(c, slot):
        c0 = pl.multiple_of(c * FFN_CHUNK, FFN_CHUNK)
        for half, off in enumerate((0, F)):
            u_ref[slot, half] = _dot(x_ext, wup_ref[:, pl.ds(off + c0, FFN_CHUNK)])

    def down(c, slot):
        c0 = pl.multiple_of(c * FFN_CHUNK, FFN_CHUNK)
        halves = []
        for half, off in enumerate((0, F)):
            w = cw_ref[:, pl.ds(off + c0, FFN_CHUNK)]
            b = cb_ref[:, pl.ds(off + c0, FFN_CHUNK)]
            halves.append(b + w[0:1] * u_ref[slot, half, HALO - 1:HALO - 1 + tm, :]
                          + w[1:2] * u_ref[slot, half, HALO:HALO + tm, :]
                          + w[2:3] * u_ref[slot, half, HALO + 1:HALO + 1 + tm, :])
        hid = (jax.nn.gelu(halves[0]) * halves[1]).astype(BF16)
        acc_ref[...] += _dot(hid, wdn_ref[pl.ds(c0, FFN_CHUNK), :])

    def pair(p, carry):
        up(2 * p + 1, 1)
        down(2 * p, 0)
        up(2 * p + 2, 0)
        down(2 * p + 1, 1)
        return carry

    up(0, 0)
    lax.fori_loop(0, (n - 1) // 2, pair, 0)
    down(n - 1, 0)
    o_ref[0] = hm + _rms(acc_ref[...], g3_ref[...])


def _conv_ffn(h, g2, w_up, conv_w, conv_b, w_down, g3, tm):
    B, S, D = h.shape
    F = w_down.shape[0]
    assert F % FFN_CHUNK == 0 and (F // FFN_CHUNK) % 2 == 1
    nb = tm // HALO
    body = functools.partial(_ffn_body, tm, F)
    return pl.pallas_call(
        body, grid=(B, S // tm),
        in_specs=[
            pl.BlockSpec((1, HALO, D), lambda b, i: (b, jnp.maximum(i * nb - 1, 0), 0)),
            pl.BlockSpec((1, tm, D), lambda b, i: (b, i, 0)),
            pl.BlockSpec((1, HALO, D), lambda b, i: (b, jnp.minimum((i + 1) * nb, S // HALO - 1), 0)),
            _resident((1, D)), _resident(w_up.shape), _resident(conv_w.shape),
            _resident((1, 2 * F)), _resident(w_down.shape), _resident((1, D)),
        ],
        out_specs=pl.BlockSpec((1, tm, D), lambda b, i: (b, i, 0)),
        out_shape=jax.ShapeDtypeStruct((B, S, D), F32),
        scratch_shapes=[pltpu.VMEM((tm, D), F32),
                        pltpu.VMEM((2, 2, tm + 2 * HALO, FFN_CHUNK), F32)],
        compiler_params=_params("parallel", "parallel"), name="conv_ffn",
    )(h, h, h, g2.reshape(1, D), w_up, conv_w, conv_b.reshape(1, 2 * F), w_down, g3.reshape(1, D))


def _mla_proj_body(h_ref, g_ref, w1_ref, qg_ref, kvg_ref, wuqT_ref, wuk_ref, wuvT_ref,
                   tab_ref, cosT_ref, sinT_ref, qT_ref, k_ref, vT_ref):
    H = MLA_HEADS
    tm = h_ref.shape[1]
    a = _rms(h_ref[0], g_ref[...]).astype(BF16)
    t1 = _dot(a, w1_ref[...])
    cq = _rms(t1[:, :MLA_Q_LORA], qg_ref[...]).astype(BF16)
    ckv = _rms(t1[:, MLA_Q_LORA:MLA_Q_LORA + MLA_KV_LORA], kvg_ref[...]).astype(BF16)
    z = t1[:, MLA_Q_LORA + MLA_KV_LORA:] * tab_ref[...]
    lane = lax.broadcasted_iota(jnp.int32, z.shape, 1)
    kr = jnp.where(lane < MLA_ROPE, z + pltpu.roll(z, MLA_ROPE, 1), 0.0).astype(BF16)
    k_nope = _dot(ckv, wuk_ref[...]).astype(BF16)
    for hd in range(H):
        k_ref[0, hd] = jnp.concatenate([k_nope[:, hd * MLA_NOPE:(hd + 1) * MLA_NOPE], kr], axis=1)
    vT_ref[0] = _dot_nt(wuvT_ref[...], ckv).astype(BF16)
    scale = (MLA_NOPE + MLA_ROPE) ** -0.5 * LOG2E
    cosT, sinT = cosT_ref[...], sinT_ref[...]
    zeros = jnp.zeros((256 - MLA_NOPE - MLA_ROPE, tm), F32)
    for hd in range(H):
        blk = _dot_nt(wuqT_ref[hd * 256:(hd + 1) * 256, :], cq)
        r = blk[128:192] * cosT + blk[192:256] * sinT
        qT_ref[0, hd * 256:(hd + 1) * 256, :] = (
            jnp.concatenate([blk[:128], r, zeros], axis=0) * scale).astype(BF16)


QSUB = 256
MLA_TQ, MLA_TK, MLA_UNROLL = 512, 512, 5
DIFF_TQ, DIFF_TK, DIFF_UNROLL = 256, 256, 10


def _online_update(s, vc, m, l, acc, shift=None):
    s_max = jnp.max(s, axis=0, keepdims=True)
    if shift is None:
        m_new = jnp.maximum(m, s_max)
        p = jnp.exp2(s - m_new)
    else:
        m_new = jnp.maximum(m, s_max - shift)
        p = jnp.exp2(s - (m_new + shift))
    alpha = jnp.exp2(m - m_new)
    return (m_new, alpha * l + jnp.sum(p, axis=0, keepdims=True),
            alpha * acc + _dot(vc, p.astype(BF16)))


def _flash_init(dv):
    return (jnp.full((1, QSUB), -jnp.inf, F32), jnp.zeros((1, QSUB), F32), jnp.zeros((dv, QSUB), F32))


def _flash_body(tk, unroll, qT_ref, k_ref, vT_ref, o_ref):
    S = k_ref.shape[2]
    G = qT_ref.shape[2] // QSUB
    dv = vT_ref.shape[1]
    n = S // tk
    qs = [qT_ref[0, :, g * QSUB:(g + 1) * QSUB] for g in range(G)]

    def scores(j):
        kc = k_ref[0, 0, pl.ds(pl.multiple_of(j * tk, tk), tk), :]
        return tuple(_dot(kc, q) for q in qs)

    def consume(j, s, stats):
        vc = vT_ref[0, :, pl.ds(pl.multiple_of(j * tk, tk), tk)]
        return tuple(_online_update(s[g], vc, *stats[g]) for g in range(G))

    def step(j, carry):
        s, stats = carry
        s_next = scores(j + 1)
        return s_next, consume(j, s, stats)

    s, stats = lax.fori_loop(0, n - 1, step, (scores(0), tuple(_flash_init(dv) for _ in range(G))),
                             unroll=max(1, min(unroll, n - 1)))
    res = consume(n - 1, s, stats)
    for g, (_, l, acc) in enumerate(res):
        o_ref[0, g * QSUB:(g + 1) * QSUB, :] = (acc / l).T.astype(o_ref.dtype)


def _mla_mixer(h, g0, w_dq, qg, w_uq, w_dkv, kvg, w_ukv, w_o, g1, tm, tq, tk, unroll):
    B, S, D = h.shape
    H = MLA_HEADS
    half = MLA_ROPE // 2

    def rot_cols(w):
        return jnp.concatenate([-w[..., half:], w[..., :half]], axis=-1)

    wkr = w_dkv[:, MLA_KV_LORA:]
    w1 = jnp.concatenate([w_dq, w_dkv[:, :MLA_KV_LORA], wkr, rot_cols(wkr)], axis=1).astype(BF16)
    wq = w_uq.reshape(MLA_Q_LORA, H, MLA_NOPE + MLA_ROPE)
    wq_r = wq[..., MLA_NOPE:]
    wuqT = jnp.concatenate([wq[..., :MLA_NOPE], wq_r, rot_cols(wq_r)], axis=-1)
    wuqT = wuqT.reshape(MLA_Q_LORA, H * 256).T.astype(BF16)
    wkv = w_ukv.reshape(MLA_KV_LORA, H, MLA_NOPE + MLA_V)
    wuk = wkv[..., :MLA_NOPE].reshape(MLA_KV_LORA, H * MLA_NOPE).astype(BF16)
    wuvT = wkv[..., MLA_NOPE:].reshape(MLA_KV_LORA, H * MLA_V).T.astype(BF16)

    inv = ROPE_THETA ** (-jnp.arange(0, MLA_ROPE, 2, dtype=F32) / MLA_ROPE)
    ang = jnp.arange(S, dtype=F32)[:, None] * inv[None, :]
    cos, sin = jnp.cos(ang), jnp.sin(ang)
    cos2, sin2 = jnp.concatenate([cos, cos], 1), jnp.concatenate([sin, sin], 1)
    tab = jnp.concatenate([cos2, sin2], 1)

    qT, k, vT = pl.pallas_call(
        _mla_proj_body, grid=(B, S // tm),
        in_specs=[pl.BlockSpec((1, tm, D), lambda b, i: (b, i, 0)), _resident((1, D)),
                  _resident(w1.shape), _resident((1, MLA_Q_LORA)), _resident((1, MLA_KV_LORA)),
                  _resident(wuqT.shape), _resident(wuk.shape), _resident(wuvT.shape),
                  pl.BlockSpec((tm, 128), lambda b, i: (i, 0)),
                  pl.BlockSpec((MLA_ROPE, tm), lambda b, i: (0, i)),
                  pl.BlockSpec((MLA_ROPE, tm), lambda b, i: (0, i))],
        out_specs=[pl.BlockSpec((1, H * 256, tm), lambda b, i: (b, 0, i)),
                   pl.BlockSpec((1, H, tm, 256), lambda b, i: (b, 0, i, 0)),
                   pl.BlockSpec((1, H * MLA_V, tm), lambda b, i: (b, 0, i))],
        out_shape=[jax.ShapeDtypeStruct((B, H * 256, S), BF16),
                   jax.ShapeDtypeStruct((B, H, S, 256), BF16),
                   jax.ShapeDtypeStruct((B, H * MLA_V, S), BF16)],
        compiler_params=_params("parallel", "parallel"), name="mla_proj",
    )(h, g0.reshape(1, D), w1, qg.reshape(1, -1), kvg.reshape(1, -1), wuqT, wuk, wuvT,
      tab, cos2.T, sin2.T)

    o = pl.pallas_call(
        functools.partial(_flash_body, tk, unroll), grid=(B, H, S // tq),
        in_specs=[pl.BlockSpec((1, 256, tq), lambda b, hd, i: (b, hd, i)),
                  pl.BlockSpec((1, 1, S, 256), lambda b, hd, i: (b, hd, 0, 0)),
                  pl.BlockSpec((1, MLA_V, S), lambda b, hd, i: (b, hd, 0))],
        out_specs=pl.BlockSpec((1, tq, MLA_V), lambda b, hd, i: (b, i, hd)),
        out_shape=jax.ShapeDtypeStruct((B, S, H * MLA_V), BF16),
        compiler_params=_params("parallel", "parallel", "arbitrary"), name="mla_flash",
    )(qT, k, vT)
    return _out_proj(o, w_o.astype(BF16), g1, h, tm)


def _diff_flash_body(tk, unroll, lam_init, slope_ref, parts_ref, qT_ref, k_ref, vT_ref, lamp_ref, sg_ref,
                     o_ref):
    S = k_ref.shape[1]
    d = DIFF_HD
    n = S // tk
    hd = pl.program_id(1)
    jd = pl.program_id(2)
    q0 = jd * tk
    slope = slope_ref[hd]
    s1, s2, s3 = parts_ref[3 * hd], parts_ref[3 * hd + 1], parts_ref[3 * hd + 2]

    qT = qT_ref[0]
    row = lax.broadcasted_iota(jnp.int32, (2 * d, QSUB), 0)
    zero = jnp.zeros_like(qT)
    qs = (jnp.where(row < d, qT, zero), jnp.where(row >= d, qT, zero))

    lane = lax.broadcasted_iota(jnp.int32, (tk, 2 * d), 1)
    rowf = lax.broadcasted_iota(jnp.int32, (tk, 2 * d), 0).astype(F32)
    k_aug = jnp.where(lane == 0, s1, jnp.where(lane == 1, s2, jnp.where(lane == 2, s3,
                      jnp.where(lane < 6, rowf, 0.0)))).astype(BF16)
    arow = lax.broadcasted_iota(jnp.int32, (16, QSUB), 0)
    colf = lax.broadcasted_iota(jnp.int32, (16, QSUB), 1).astype(F32)
    q_aug = jnp.where(arow < 3, -colf, jnp.where(arow == 3, s1, jnp.where(arow == 4, s2,
                      jnp.where(arow == 5, s3, 0.0))))
    q_pad = jnp.zeros((2 * d - 16, QSUB), BF16)

    def block(jj):
        jj = lax.convert_element_type(jj, jnp.int32)
        return jj + lax.convert_element_type(jj >= jd, jnp.int32)

    def scores(jj):
        j = block(jj)
        kc = k_ref[0, pl.ds(pl.multiple_of(j * tk, tk), tk), :]
        kc = jnp.concatenate([kc, k_aug], axis=1)
        sgn = jnp.where(j < jd, 1.0, -1.0)
        aug = (q_aug * sgn).astype(BF16)
        return tuple(_dot(kc, jnp.concatenate([q, aug, q_pad], axis=0)) for q in qs)

    def consume(jj, t, stats):
        j = block(jj)
        vc = vT_ref[0, :, pl.ds(pl.multiple_of(j * tk, tk), tk)]
        kappa = slope * lax.convert_element_type(jnp.abs(q0 - j * tk), F32)
        return tuple(_online_update(t[c], vc, *stats[c], shift=kappa) for c in range(2))

    rel = (lax.broadcasted_iota(jnp.int32, (tk, QSUB), 1)
           - lax.broadcasted_iota(jnp.int32, (tk, QSUB), 0))
    bias_d = slope * jnp.abs(rel).astype(F32)
    kd = k_ref[0, pl.ds(pl.multiple_of(q0, tk), tk), :]
    vd = vT_ref[0, :, pl.ds(pl.multiple_of(q0, tk), tk)]
    sd = tuple(_dot(kd, q) for q in qs)
    t = scores(0) if n > 1 else None
    stats = tuple(_online_update(s - bias_d, vd, *_flash_init(2 * d)) for s in sd)

    if n > 1:
        def step(jj, carry):
            t, st = carry
            t_next = scores(jj + 1)
            return t_next, consume(jj, t, st)

        t, stats = lax.fori_loop(0, n - 2, step, (t, stats), unroll=max(1, min(unroll, n - 2)))
        stats = consume(n - 2, t, stats)

    lp = lamp_ref[...]
    lam = (jnp.exp(jnp.sum(lp[0:1] * lp[1:2], axis=1, keepdims=True))
           - jnp.exp(jnp.sum(lp[2:3] * lp[3:4], axis=1, keepdims=True)) + lam_init)
    gain = sg_ref[...] * (1.0 - lam_init)
    (_, l1, a1), (_, l2, a2) = stats
    oT = a1 / l1 - lam * (a2 / l2)
    oT = oT * lax.rsqrt(jnp.mean(oT * oT, axis=0, keepdims=True) + EPS) * gain
    o_ref[0] = oT.T.astype(o_ref.dtype)


def _diff_mixer(h, g0, w_qkv, lam_p, subln_g, w_o, g1, layer_idx, tm, tq, tk, unroll):
    B, S, D = h.shape
    H, d = DIFF_HEADS, DIFF_HD
    scale = d ** -0.5 * LOG2E
    wqT = w_qkv[:, :D].T.astype(BF16)
    wk = w_qkv[:, D:2 * D].astype(BF16)
    wvT = w_qkv[:, 2 * D:].T.astype(BF16)
    k, qT, vT = _norm_proj(h, g0, [(wk, None, 1.0, BF16)],
                           [(wqT, None, scale, BF16), (wvT, None, 1.0, BF16)], tm)
    lam_init = 0.8 - 0.6 * math.exp(-0.3 * layer_idx)
    slopes = (2.0 ** (-8.0 * np.arange(1, H + 1, dtype=np.float32) / H)) * np.float32(LOG2E)
    p1 = slopes.astype(BF16).astype(np.float32)
    p2 = (slopes - p1).astype(BF16).astype(np.float32)
    p3 = (slopes - p1 - p2).astype(BF16).astype(np.float32)
    assert np.all(p1 + p2 + p3 == slopes)
    parts = jnp.asarray(np.stack([p1, p2, p3], axis=1).reshape(3 * H))
    slopes = jnp.asarray(slopes)
    assert tq == tk == QSUB <= 256
    o = pl.pallas_call(
        functools.partial(_diff_flash_body, tk, unroll, lam_init), grid=(B, H, S // tq),
        in_specs=[pl.BlockSpec(memory_space=pltpu.SMEM), pl.BlockSpec(memory_space=pltpu.SMEM),
                  pl.BlockSpec((1, 2 * d, tq), lambda b, hd, i: (b, hd, i)),
                  pl.BlockSpec((1, S, 2 * d), lambda b, hd, i: (b, 0, hd)),
                  pl.BlockSpec((1, 2 * d, S), lambda b, hd, i: (b, hd, 0)),
                  pl.BlockSpec((4, d), lambda b, hd, i: (0, 0)),
                  pl.BlockSpec((2 * d, 1), lambda b, hd, i: (0, 0))],
        out_specs=pl.BlockSpec((1, tq, 2 * d), lambda b, hd, i: (b, i, hd)),
        out_shape=jax.ShapeDtypeStruct((B, S, H * 2 * d), BF16),
        compiler_params=_params("parallel", "parallel", "arbitrary"), name="diff_flash",
    )(slopes, parts, qT, k, vT, lam_p.astype(F32), subln_g.reshape(2 * d, 1).astype(F32))
    return _out_proj(o, w_o.astype(BF16), g1, h, tm)


def _mlstm_body(q_ref, kT_ref, v_ref, g_ref, gT_ref, o_ref, c_ref, m_ref):
    H, L, dv = MLSTM_HEADS, CHUNK, MLSTM_V
    sgn = 1 - 2 * pl.program_id(0)

    @pl.when(pl.program_id(2) == 0)
    def _():
        c_ref[...] = jnp.zeros_like(c_ref)
        m_ref[...] = jnp.zeros_like(m_ref)

    r = lax.broadcasted_iota(jnp.int32, (L, L), 0)
    c = lax.broadcasted_iota(jnp.int32, (L, L), 1)
    causal = (r - c) * sgn >= 0
    causal_t = (c - r) * sgn >= 0
    lane = lax.broadcasted_iota(jnp.int32, (L, dv), 1)
    ones_blk = jnp.where(lane == 0, 1.0, 0.0).astype(BF16)
    g = g_ref[0, 0]
    gT = gT_ref[0, 0]
    for hd in range(H):
        i_row = gT[hd:hd + 1, :]
        lf_row = _log_sigmoid(gT[H + hd:H + hd + 1, :])
        lf_col = _log_sigmoid(g[:, H + hd:H + hd + 1])
        bcum_col = jnp.sum(jnp.where(causal, lf_row, 0.0), axis=1, keepdims=True)
        bcum_row = jnp.sum(jnp.where(causal_t, lf_col, 0.0), axis=0, keepdims=True)
        b_last = jnp.sum(lf_row, axis=1, keepdims=True)
        m_prev = m_ref[hd][:, 0:1]

        dmat = jnp.where(causal, bcum_col - bcum_row + i_row, -jnp.inf)
        m_inter = bcum_col + m_prev
        m_t = jnp.maximum(jnp.max(dmat, axis=1, keepdims=True), m_inter)
        w_intra = jnp.exp(dmat - m_t)
        w_inter = jnp.exp(m_inter - m_t)

        qc = q_ref[0, :, hd * 128:(hd + 1) * 128]
        kT = kT_ref[0, hd * 128:(hd + 1) * 128, :]
        v_ext = jnp.concatenate([v_ref[0, :, hd * dv:(hd + 1) * dv], ones_blk], axis=1)
        c_ext = c_ref[hd]
        sqk = (_dot(qc, kT) * w_intra).astype(BF16)
        nd = w_inter * _dot(qc, c_ext.astype(BF16)) + _dot(sqk, v_ext)
        den = nd[:, dv:dv + 1]
        o_ref[0, 0, :, hd * dv:(hd + 1) * dv] = (
            nd[:, :dv] / jnp.maximum(jnp.abs(den), jnp.exp(-m_t)))

        g_s = b_last - bcum_row + i_row
        m_new = jnp.maximum(b_last + m_prev, jnp.max(g_s, axis=1, keepdims=True))
        w_s = jnp.exp(g_s - m_new)
        decay = jnp.exp(b_last + m_prev - m_new)
        c_ref[hd] = decay * c_ext + _dot((kT * w_s).astype(BF16), v_ext)
        m_ref[hd] = jnp.broadcast_to(m_new, m_ref.shape[1:])


def _mlstm_post_body(hf_ref, hb_ref, og_ref, ng_ref, w_ref, g_ref, h_ref, o_ref):
    H, dv = MLSTM_HEADS, MLSTM_V
    hs = hf_ref[0, 0] + hb_ref[0, 0]
    parts = []
    for hd in range(H):
        x = hs[:, hd * dv:(hd + 1) * dv]
        xc = x - jnp.mean(x, axis=1, keepdims=True)
        parts.append(xc * lax.rsqrt(jnp.mean(xc * xc, axis=1, keepdims=True) + EPS))
    y = jnp.concatenate(parts, axis=1) * ng_ref[...] * _sigmoid(og_ref[0].astype(F32))
    o_ref[0] = h_ref[0] + _rms(_dot(y.astype(BF16), w_ref[...]), g_ref[...])


def _mlstm_mixer(h, g0, w_in, b_gates, norm_g, w_out, g1, tm):
    B, S, D = h.shape
    H, dk, dv = MLSTM_HEADS, MLSTM_QK, MLSTM_V
    o1, o2, o3, o4 = H * dk, 2 * H * dk, 2 * H * dk + H * dv, 2 * H * dk + 2 * H * dv

    def pad_heads(w):
        w = w.reshape(D, H, dk)
        return jnp.concatenate([w, jnp.zeros((D, H, 128 - dk), w.dtype)], -1).reshape(D, H * 128)

    wq = pad_heads(w_in[:, :o1]).astype(BF16)
    wkT = pad_heads(w_in[:, o1:o2]).T.astype(BF16)
    wv = w_in[:, o2:o3].astype(BF16)
    wog = w_in[:, o3:o4].astype(BF16)
    wg = w_in[:, o4:].astype(BF16)
    q, v, og, g, kT, gT = _norm_proj(
        h, g0,
        [(wq, None, 1.0, BF16), (wv, None, 1.0, BF16), (wog, None, 1.0, BF16), (wg, b_gates, 1.0, F32)],
        [(wkT, None, dk ** -0.5, BF16), (wg.T, b_gates, 1.0, F32)], tm)
    g = jnp.moveaxis(g.reshape(B, S, 2, 2 * H), 2, 0)
    gT = jnp.moveaxis(gT.reshape(B, 2, 2 * H, S), 1, 0)
    NC = S // CHUNK

    def ck(dr, ci):
        return ci + dr * (NC - 1 - 2 * ci)

    hout = pl.pallas_call(
        _mlstm_body, grid=(2, B, NC),
        in_specs=[pl.BlockSpec((1, CHUNK, H * 128), lambda dr, b, ci: (b, ck(dr, ci), 0)),
                  pl.BlockSpec((1, H * 128, CHUNK), lambda dr, b, ci: (b, 0, ck(dr, ci))),
                  pl.BlockSpec((1, CHUNK, H * dv), lambda dr, b, ci: (b, ck(dr, ci), 0)),
                  pl.BlockSpec((1, 1, CHUNK, 2 * H), lambda dr, b, ci: (dr, b, ck(dr, ci), 0)),
                  pl.BlockSpec((1, 1, 2 * H, CHUNK), lambda dr, b, ci: (dr, b, 0, ck(dr, ci)))],
        out_specs=pl.BlockSpec((1, 1, CHUNK, H * dv), lambda dr, b, ci: (dr, b, ck(dr, ci), 0)),
        out_shape=jax.ShapeDtypeStruct((2, B, S, H * dv), F32),
        scratch_shapes=[pltpu.VMEM((H, 128, 2 * dv), F32), pltpu.VMEM((H, 1, 128), F32)],
        compiler_params=_params("parallel", "parallel", "arbitrary"), name="mlstm_scan",
    )(q, kT, v, g, gT)

    return pl.pallas_call(
        _mlstm_post_body, grid=(B, S // tm),
        in_specs=[pl.BlockSpec((1, 1, tm, H * dv), lambda b, i: (0, b, i, 0)),
                  pl.BlockSpec((1, 1, tm, H * dv), lambda b, i: (1, b, i, 0)),
                  pl.BlockSpec((1, tm, H * dv), lambda b, i: (b, i, 0)),
                  _resident((1, H * dv)), _resident((H * dv, D)), _resident((1, D)),
                  pl.BlockSpec((1, tm, D), lambda b, i: (b, i, 0))],
        out_specs=pl.BlockSpec((1, tm, D), lambda b, i: (b, i, 0)),
        out_shape=jax.ShapeDtypeStruct((B, S, D), F32),
        compiler_params=_params("parallel", "parallel"), name="mlstm_post",
    )(hout, hout, og, norm_g.reshape(1, -1), w_out.astype(BF16), g1.reshape(1, D), h)


def _ret_body(q_ref, kT_ref, v_ref, dl_ref, o_ref, r_ref):
    H, L, dk, dv = RET_HEADS, CHUNK, RET_QK, RET_V
    dr = pl.program_id(0)

    @pl.when(pl.program_id(2) == 0)
    def _():
        r_ref[...] = jnp.zeros_like(r_ref)

    r = lax.broadcasted_iota(jnp.int32, (L, L), 0)
    c = lax.broadcasted_iota(jnp.int32, (L, L), 1)
    rel = (r - c) * (1 - 2 * dr)
    relf = jnp.maximum(rel, 0).astype(F32)
    t_col = lax.broadcasted_iota(jnp.int32, (L, 1), 0)
    s_row = lax.broadcasted_iota(jnp.int32, (1, L), 1)
    xi_pow = (t_col + 1 + dr * (L - 1 - 2 * t_col)).astype(F32)
    zeta_pow = (L - 1 - s_row + dr * (2 * s_row - (L - 1))).astype(F32)
    lg = _log_sigmoid(dl_ref[0])
    for hd in range(H):
        lgh = lg[:, hd:hd + 1]
        dmask = jnp.where(rel >= 0, jnp.exp(relf * lgh), 0.0)
        qc = q_ref[0, :, hd * dk:(hd + 1) * dk]
        kT = kT_ref[0, hd * dk:(hd + 1) * dk, :]
        vc = v_ref[0, :, hd * dv:(hd + 1) * dv]
        state = r_ref[hd]
        inner = _dot((_dot(qc, kT) * dmask).astype(BF16), vc)
        cross = _dot(qc, state.astype(BF16)) * jnp.exp(xi_pow * lgh)
        o_ref[0, 0, :, hd * dv:(hd + 1) * dv] = inner + cross
        kz = (kT * jnp.exp(zeta_pow * lgh)).astype(BF16)
        r_ref[hd] = jnp.exp(L * lgh) * state + _dot(kz, vc)


def _ret_post_body(yf_ref, yb_ref, gate_ref, ng_ref, w_ref, g_ref, h_ref, o_ref):
    H, dv = RET_HEADS, RET_V
    ys = yf_ref[0, 0] + yb_ref[0, 0]
    parts = []
    for hd in range(H):
        x = ys[:, hd * dv:(hd + 1) * dv]
        xc = x - jnp.mean(x, axis=1, keepdims=True)
        parts.append(xc * lax.rsqrt(jnp.mean(xc * xc, axis=1, keepdims=True) + EPS))
    gate = gate_ref[0].astype(F32)
    y = jnp.concatenate(parts, axis=1) * ng_ref[...] * (gate * _sigmoid(gate))
    o_ref[0] = h_ref[0] + _rms(_dot(y.astype(BF16), w_ref[...]), g_ref[...])


def _ret_mixer(h, g0, w_in, decay_logit, norm_g, w_o, g1, tm):
    B, S, D = h.shape
    H, dk, dv = RET_HEADS, RET_QK, RET_V
    o1, o2, o3 = H * dk, 2 * H * dk, 2 * H * dk + H * dv
    wq = w_in[:, :o1].astype(BF16)
    wkT = w_in[:, o1:o2].T.astype(BF16)
    wv = w_in[:, o2:o3].astype(BF16)
    wgate = w_in[:, o3:].astype(BF16)
    q, v, gate, kT = _norm_proj(
        h, g0, [(wq, None, 1.0, BF16), (wv, None, 1.0, BF16), (wgate, None, 1.0, BF16)],
        [(wkT, None, dk ** -0.5, BF16)], tm)
    NC = S // CHUNK

    def ck(dr, ci):
        return ci + dr * (NC - 1 - 2 * ci)

    y = pl.pallas_call(
        _ret_body, grid=(2, B, NC),
        in_specs=[pl.BlockSpec((1, CHUNK, H * dk), lambda dr, b, ci: (b, ck(dr, ci), 0)),
                  pl.BlockSpec((1, H * dk, CHUNK), lambda dr, b, ci: (b, 0, ck(dr, ci))),
                  pl.BlockSpec((1, CHUNK, H * dv), lambda dr, b, ci: (b, ck(dr, ci), 0)),
                  pl.BlockSpec((1, 1, H), lambda dr, b, ci: (dr, 0, 0))],
        out_specs=pl.BlockSpec((1, 1, CHUNK, H * dv), lambda dr, b, ci: (dr, b, ck(dr, ci), 0)),
        out_shape=jax.ShapeDtypeStruct((2, B, S, H * dv), F32),
        scratch_shapes=[pltpu.VMEM((H, dk, dv), F32)],
        compiler_params=_params("parallel", "parallel", "arbitrary"), name="ret_scan",
    )(q, kT, v, decay_logit.astype(F32).reshape(2, 1, H))

    return pl.pallas_call(
        _ret_post_body, grid=(B, S // tm),
        in_specs=[pl.BlockSpec((1, 1, tm, H * dv), lambda b, i: (0, b, i, 0)),
                  pl.BlockSpec((1, 1, tm, H * dv), lambda b, i: (1, b, i, 0)),
                  pl.BlockSpec((1, tm, H * dv), lambda b, i: (b, i, 0)),
                  _resident((1, H * dv)), _resident((H * dv, D)), _resident((1, D)),
                  pl.BlockSpec((1, tm, D), lambda b, i: (b, i, 0))],
        out_specs=pl.BlockSpec((1, tm, D), lambda b, i: (b, i, 0)),
        out_shape=jax.ShapeDtypeStruct((B, S, D), F32),
        compiler_params=_params("parallel", "parallel"), name="ret_post",
    )(y, y, gate, norm_g.reshape(1, -1), w_o.astype(BF16), g1.reshape(1, D), h)


def _tile(s, want):
    return min(want, s)


def kernel(x, norm_g, ffn_w_up, ffn_conv_w, ffn_conv_b, ffn_w_down, mla_w_dq, mla_q_norm_g, mla_w_uq, mla_w_dkv, mla_kv_norm_g, mla_w_ukv, mla_w_o, diff_w_qkv, diff_lambda, diff_subln_g, diff_w_o, mlstm_w_in, mlstm_b_gates, mlstm_norm_g, mlstm_w_out, ret_w_in, ret_decay_logit, ret_norm_g, ret_w_o):
    depth = norm_g.shape[0]
    S = x.shape[1]
    tm = _tile(S, 512)
    mla_t = (_tile(S, MLA_TQ), _tile(S, MLA_TK), MLA_UNROLL)
    diff_t = (_tile(S, DIFF_TQ), _tile(S, DIFF_TK), DIFF_UNROLL)
    h = x
    for i in range(depth):
        kind, j = i % 4, i // 4
        g0, g1, g2, g3 = norm_g[i, 0], norm_g[i, 1], norm_g[i, 2], norm_g[i, 3]
        if kind == 0:
            h = _mla_mixer(h, g0, mla_w_dq[j], mla_q_norm_g[j], mla_w_uq[j], mla_w_dkv[j],
                           mla_kv_norm_g[j], mla_w_ukv[j], mla_w_o[j], g1, tm, *mla_t)
        elif kind == 1:
            h = _diff_mixer(h, g0, diff_w_qkv[j], diff_lambda[j], diff_subln_g[j], diff_w_o[j],
                            g1, i, tm, *diff_t)
        elif kind == 2:
            h = _mlstm_mixer(h, g0, mlstm_w_in[j], mlstm_b_gates[j], mlstm_norm_g[j],
                             mlstm_w_out[j], g1, tm)
        else:
            h = _ret_mixer(h, g0, ret_w_in[j], ret_decay_logit[j], ret_norm_g[j], ret_w_o[j], g1, tm)
        h = _conv_ffn(h, g2, ffn_w_up[i].astype(BF16), ffn_conv_w[i], ffn_conv_b[i],
                      ffn_w_down[i].astype(BF16), g3, tm)
    return h
```

```python
import functools
import math

import jax
import jax.numpy as jnp
import numpy as np
from jax import lax
from jax.experimental import pallas as pl
from jax.experimental.pallas import tpu as pltpu

F32 = jnp.float32
BF16 = jnp.bfloat16

EPS = 1e-6
LOG2E = 1.4426950408889634
CHUNK = 128
HALO = 8

MLA_HEADS, MLA_NOPE, MLA_ROPE, MLA_V = 8, 128, 64, 128
MLA_Q_LORA, MLA_KV_LORA = 384, 256
ROPE_THETA = 10000.0
DIFF_HEADS, DIFF_HD = 8, 64
MLSTM_HEADS, MLSTM_QK, MLSTM_V = 8, 64, 128
RET_HEADS, RET_QK, RET_V = 4, 256, 512
CONV_W = 3

VMEM_LIMIT = 56 * 1024 * 1024


def _params(*sem):
    return pltpu.CompilerParams(dimension_semantics=sem, vmem_limit_bytes=VMEM_LIMIT)


def _resident(shape):
    nd = len(shape)
    return pl.BlockSpec(shape, lambda *_: (0,) * nd, pipeline_mode=pl.Buffered(1))


def _rms(x, g):
    return x * lax.rsqrt(jnp.mean(x * x, axis=-1, keepdims=True) + EPS) * g


def _dot(a, b):
    return jnp.dot(a, b, preferred_element_type=F32)


def _dot_nt(a, b):
    return lax.dot_general(a, b, (((1,), (1,)), ((), ())), preferred_element_type=F32)


def _log_sigmoid(x):
    return jnp.minimum(x, 0.0) - jnp.log1p(jnp.exp(-jnp.abs(x)))


def _sigmoid(x):
    return 1.0 / (1.0 + jnp.exp(-x))


COL_CHUNK = 512


def _proj_body(n_nat, n_t, nat_scales, t_scales, *refs):
    h_ref, g_ref = refs[0], refs[1]
    nat_w = refs[2:2 + n_nat]
    nat_b = refs[2 + n_nat:2 + 2 * n_nat]
    t_w = refs[2 + 2 * n_nat:2 + 2 * n_nat + n_t]
    t_b = refs[2 + 2 * n_nat + n_t:2 + 2 * n_nat + 2 * n_t]
    outs = refs[2 + 2 * n_nat + 2 * n_t:]
    nat_o, t_o = outs[:n_nat], outs[n_nat:]

    xn = _rms(h_ref[0], g_ref[...]).astype(BF16)
    for w_ref, b_ref, o_ref, sc in zip(nat_w, nat_b, nat_o, nat_scales):
        m = w_ref.shape[1]
        for c0 in range(0, m, COL_CHUNK):
            cw = min(COL_CHUNK, m - c0)
            y = _dot(xn, w_ref[:, c0:c0 + cw]) + b_ref[:, c0:c0 + cw]
            if sc != 1.0:
                y = y * sc
            o_ref[0, :, c0:c0 + cw] = y.astype(o_ref.dtype)
    for w_ref, b_ref, o_ref, sc in zip(t_w, t_b, t_o, t_scales):
        m = w_ref.shape[0]
        for r0 in range(0, m, COL_CHUNK):
            rw = min(COL_CHUNK, m - r0)
            y = _dot_nt(w_ref[r0:r0 + rw, :], xn) + b_ref[r0:r0 + rw, :]
            if sc != 1.0:
                y = y * sc
            o_ref[0, r0:r0 + rw, :] = y.astype(o_ref.dtype)


def _norm_proj(h, g, nat, tr, tm):
    B, S, D = h.shape
    args = [h, g.reshape(1, D)]
    specs = [pl.BlockSpec((1, tm, D), lambda b, i: (b, i, 0)), _resident((1, D))]
    for w, _, _, _ in nat:
        args.append(w)
        specs.append(_resident(w.shape))
    for w, bias, _, _ in nat:
        m = w.shape[1]
        args.append(jnp.zeros((1, m), F32) if bias is None else bias.reshape(1, m).astype(F32))
        specs.append(_resident((1, m)))
    for w, _, _, _ in tr:
        args.append(w)
        specs.append(_resident(w.shape))
    for w, bias, _, _ in tr:
        m = w.shape[0]
        args.append(jnp.zeros((m, 1), F32) if bias is None else bias.reshape(m, 1).astype(F32))
        specs.append(_resident((m, 1)))
    out_shape, out_specs = [], []
    for w, _, _, dt in nat:
        m = w.shape[1]
        out_shape.append(jax.ShapeDtypeStruct((B, S, m), dt))
        out_specs.append(pl.BlockSpec((1, tm, m), lambda b, i: (b, i, 0)))
    for w, _, _, dt in tr:
        m = w.shape[0]
        out_shape.append(jax.ShapeDtypeStruct((B, m, S), dt))
        out_specs.append(pl.BlockSpec((1, m, tm), lambda b, i: (b, 0, i)))
    body = functools.partial(_proj_body, len(nat), len(tr),
                             tuple(x[2] for x in nat), tuple(x[2] for x in tr))
    return pl.pallas_call(
        body, grid=(B, S // tm), in_specs=specs, out_specs=out_specs, out_shape=out_shape,
        compiler_params=_params("parallel", "parallel"), name="norm_proj")(*args)


def _out_body(a_ref, w_ref, g_ref, h_ref, o_ref):
    y = _dot(a_ref[0], w_ref[...])
    o_ref[0] = h_ref[0] + _rms(y, g_ref[...])


def _out_proj(a, w, g, h, tm):
    B, S, K = a.shape
    D = h.shape[-1]
    return pl.pallas_call(
        _out_body, grid=(B, S // tm),
        in_specs=[pl.BlockSpec((1, tm, K), lambda b, i: (b, i, 0)), _resident(w.shape),
                  _resident((1, D)), pl.BlockSpec((1, tm, D), lambda b, i: (b, i, 0))],
        out_specs=pl.BlockSpec((1, tm, D), lambda b, i: (b, i, 0)),
        out_shape=jax.ShapeDtypeStruct((B, S, D), F32),
        compiler_params=_params("parallel", "parallel"), name="out_proj")(a, w, g.reshape(1, D), h)


FFN_CHUNK = 256


def _ffn_body(tm, F, hp_ref, hm_ref, hn_ref, g2_ref, wup_ref, cw_ref, cb_ref, wdn_ref, g3_ref,
              o_ref, acc_ref, u_ref):
    i = pl.program_id(1)
    last = pl.num_programs(1) - 1
    g2 = g2_ref[...]
    hm = hm_ref[0]
    xp = _rms(hp_ref[0], g2) * jnp.where(i == 0, 0.0, 1.0)
    xn_ = _rms(hn_ref[0], g2) * jnp.where(i == last, 0.0, 1.0)
    x_ext = jnp.concatenate([xp, _rms(hm, g2), xn_], axis=0).astype(BF16)

    acc_ref[...] = jnp.zeros_like(acc_ref)

    n = F // FFN_CHUNK

    def up(c, slot):
        c0 = pl.multiple_of(c * FFN_CHUNK, FFN_CHUNK)
        for half, off in enumerate((0, F)):
            u_ref[slot, half] = _dot(x_ext, wup_ref[:, pl.ds(off + c0, FFN_CHUNK)])

    def down(c, slot):
        c0 = pl.multiple_of(c * FFN_CHUNK, FFN_CHUNK)
        halves = []
        for half, off in enumerate((0, F)):
            w = cw_ref[:, pl.ds(off + c0, FFN_CHUNK)]
            b = cb_ref[:, pl.ds(off + c0, FFN_CHUNK)]
            halves.append(b + w[0:1] * u_ref[slot, half, HALO - 1:HALO - 1 + tm, :]
                          + w[1:2] * u_ref[slot, half, HALO:HALO + tm, :]
                          + w[2:3] * u_ref[slot, half, HALO + 1:HALO + 1 + tm, :])
        hid = (jax.nn.gelu(halves[0]) * halves[1]).astype(BF16)
        acc_ref[...] += _dot(hid, wdn_ref[pl.ds(c0, FFN_CHUNK), :])

    def pair(p, carry):
        up(2 * p + 1, 1)
        down(2 * p, 0)
        up(2 * p + 2, 0)
        down(2 * p + 1, 1)
        return carry

    up(0, 0)
    lax.fori_loop(0, (n - 1) // 2, pair, 0)
    down(n - 1, 0)
    o_ref[0] = hm + _rms(acc_ref[...], g3_ref[...])


def _conv_ffn(h, g2, w_up, conv_w, conv_b, w_down, g3, tm):
    B, S, D = h.shape
    F = w_down.shape[0]
    assert F % FFN_CHUNK == 0 and (F // FFN_CHUNK) % 2 == 1
    nb = tm // HALO
    body = functools.partial(_ffn_body, tm, F)
    return pl.pallas_call(
        body, grid=(B, S // tm),
        in_specs=[
            pl.BlockSpec((1, HALO, D), lambda b, i: (b, jnp.maximum(i * nb - 1, 0), 0)),
            pl.BlockSpec((1, tm, D), lambda b, i: (b, i, 0)),
            pl.BlockSpec((1, HALO, D), lambda b, i: (b, jnp.minimum((i + 1) * nb, S // HALO - 1), 0)),
            _resident((1, D)), _resident(w_up.shape), _resident(conv_w.shape),
            _resident((1, 2 * F)), _resident(w_down.shape), _resident((1, D)),
        ],
        out_specs=pl.BlockSpec((1, tm, D), lambda b, i: (b, i, 0)),
        out_shape=jax.ShapeDtypeStruct((B, S, D), F32),
        scratch_shapes=[pltpu.VMEM((tm, D), F32),
                        pltpu.VMEM((2, 2, tm + 2 * HALO, FFN_CHUNK), F32)],
        compiler_params=_params("parallel", "parallel"), name="conv_ffn",
    )(h, h, h, g2.reshape(1, D), w_up, conv_w, conv_b.reshape(1, 2 * F), w_down, g3.reshape(1, D))


def _mla_proj_body(h_ref, g_ref, w1_ref, qg_ref, kvg_ref, wuqT_ref, wuk_ref, wuvT_ref,
                   tab_ref, cosT_ref, sinT_ref, qT_ref, k_ref, vT_ref):
    H = MLA_HEADS
    tm = h_ref.shape[1]
    a = _rms(h_ref[0], g_ref[...]).astype(BF16)
    t1 = _dot(a, w1_ref[...])
    cq = _rms(t1[:, :MLA_Q_LORA], qg_ref[...]).astype(BF16)
    ckv = _rms(t1[:, MLA_Q_LORA:MLA_Q_LORA + MLA_KV_LORA], kvg_ref[...]).astype(BF16)
    z = t1[:, MLA_Q_LORA + MLA_KV_LORA:] * tab_ref[...]
    lane = lax.broadcasted_iota(jnp.int32, z.shape, 1)
    kr = jnp.where(lane < MLA_ROPE, z + pltpu.roll(z, MLA_ROPE, 1), 0.0).astype(BF16)
    k_nope = _dot(ckv, wuk_ref[...]).astype(BF16)
    for hd in range(H):
        k_ref[0, hd] = jnp.concatenate([k_nope[:, hd * MLA_NOPE:(hd + 1) * MLA_NOPE], kr], axis=1)
    vT_ref[0] = _dot_nt(wuvT_ref[...], ckv).astype(BF16)
    scale = (MLA_NOPE + MLA_ROPE) ** -0.5 * LOG2E
    cosT, sinT = cosT_ref[...], sinT_ref[...]
    zeros = jnp.zeros((256 - MLA_NOPE - MLA_ROPE, tm), F32)
    for hd in range(H):
        blk = _dot_nt(wuqT_ref[hd * 256:(hd + 1) * 256, :], cq)
        r = blk[128:192] * cosT + blk[192:256] * sinT
        qT_ref[0, hd * 256:(hd + 1) * 256, :] = (
            jnp.concatenate([blk[:128], r, zeros], axis=0) * scale).astype(BF16)


QSUB = 256
MLA_TQ, MLA_TK, MLA_UNROLL = 1024, 512, 1
DIFF_TQ, DIFF_TK, DIFF_UNROLL = 256, 256, 5


def _online_update(s, vc, m, l, acc, shift=None):
    s_max = jnp.max(s, axis=0, keepdims=True)
    if shift is None:
        m_new = jnp.maximum(m, s_max)
        p = jnp.exp2(s - m_new)
    else:
        m_new = jnp.maximum(m, s_max - shift)
        p = jnp.exp2(s - (m_new + shift))
    alpha = jnp.exp2(m - m_new)
    return (m_new, alpha * l + jnp.sum(p, axis=0, keepdims=True),
            alpha * acc + _dot(vc, p.astype(BF16)))


def _flash_init(dv):
    return (jnp.full((1, QSUB), -jnp.inf, F32), jnp.zeros((1, QSUB), F32), jnp.zeros((dv, QSUB), F32))


def _flash_body(tk, unroll, qT_ref, k_ref, vT_ref, o_ref, s_ref):
    S = k_ref.shape[2]
    G = qT_ref.shape[2] // QSUB
    dv = vT_ref.shape[1]
    n = S // tk
    qs = [qT_ref[0, :, g * QSUB:(g + 1) * QSUB] for g in range(G)]

    def scores(j, slot):
        kc = k_ref[0, 0, pl.ds(pl.multiple_of(j * tk, tk), tk), :]
        for g in range(G):
            s_ref[slot, g] = _dot(kc, qs[g])

    def consume(j, slot, stats):
        vc = vT_ref[0, :, pl.ds(pl.multiple_of(j * tk, tk), tk)]
        return tuple(_online_update(s_ref[slot, g], vc, *stats[g]) for g in range(G))

    def pair(p, stats):
        scores(2 * p + 1, 1)
        stats = consume(2 * p, 0, stats)
        scores(2 * p + 2, 0)
        return consume(2 * p + 1, 1, stats)

    stats = tuple(_flash_init(dv) for _ in range(G))
    scores(0, 0)
    npair = (n - 1) // 2
    stats = lax.fori_loop(0, npair, pair, stats, unroll=max(1, min(unroll, npair)))
    if n % 2 == 0:
        scores(n - 1, 1)
        stats = consume(n - 2, 0, stats)
        res = consume(n - 1, 1, stats)
    else:
        res = consume(n - 1, 0, stats)
    for g, (_, l, acc) in enumerate(res):
        o_ref[0, g * QSUB:(g + 1) * QSUB, :] = (acc / l).T.astype(o_ref.dtype)


def _mla_mixer(h, g0, w_dq, qg, w_uq, w_dkv, kvg, w_ukv, w_o, g1, tm, tq, tk, unroll):
    B, S, D = h.shape
    H = MLA_HEADS
    half = MLA_ROPE // 2

    def rot_cols(w):
        return jnp.concatenate([-w[..., half:], w[..., :half]], axis=-1)

    wkr = w_dkv[:, MLA_KV_LORA:]
    w1 = jnp.concatenate([w_dq, w_dkv[:, :MLA_KV_LORA], wkr, rot_cols(wkr)], axis=1).astype(BF16)
    wq = w_uq.reshape(MLA_Q_LORA, H, MLA_NOPE + MLA_ROPE)
    wq_r = wq[..., MLA_NOPE:]
    wuqT = jnp.concatenate([wq[..., :MLA_NOPE], wq_r, rot_cols(wq_r)], axis=-1)
    wuqT = wuqT.reshape(MLA_Q_LORA, H * 256).T.astype(BF16)
    wkv = w_ukv.reshape(MLA_KV_LORA, H, MLA_NOPE + MLA_V)
    wuk = wkv[..., :MLA_NOPE].reshape(MLA_KV_LORA, H * MLA_NOPE).astype(BF16)
    wuvT = wkv[..., MLA_NOPE:].reshape(MLA_KV_LORA, H * MLA_V).T.astype(BF16)

    inv = ROPE_THETA ** (-jnp.arange(0, MLA_ROPE, 2, dtype=F32) / MLA_ROPE)
    ang = jnp.arange(S, dtype=F32)[:, None] * inv[None, :]
    cos, sin = jnp.cos(ang), jnp.sin(ang)
    cos2, sin2 = jnp.concatenate([cos, cos], 1), jnp.concatenate([sin, sin], 1)
    tab = jnp.concatenate([cos2, sin2], 1)

    qT, k, vT = pl.pallas_call(
        _mla_proj_body, grid=(B, S // tm),
        in_specs=[pl.BlockSpec((1, tm, D), lambda b, i: (b, i, 0)), _resident((1, D)),
                  _resident(w1.shape), _resident((1, MLA_Q_LORA)), _resident((1, MLA_KV_LORA)),
                  _resident(wuqT.shape), _resident(wuk.shape), _resident(wuvT.shape),
                  pl.BlockSpec((tm, 128), lambda b, i: (i, 0)),
                  pl.BlockSpec((MLA_ROPE, tm), lambda b, i: (0, i)),
                  pl.BlockSpec((MLA_ROPE, tm), lambda b, i: (0, i))],
        out_specs=[pl.BlockSpec((1, H * 256, tm), lambda b, i: (b, 0, i)),
                   pl.BlockSpec((1, H, tm, 256), lambda b, i: (b, 0, i, 0)),
                   pl.BlockSpec((1, H * MLA_V, tm), lambda b, i: (b, 0, i))],
        out_shape=[jax.ShapeDtypeStruct((B, H * 256, S), BF16),
                   jax.ShapeDtypeStruct((B, H, S, 256), BF16),
                   jax.ShapeDtypeStruct((B, H * MLA_V, S), BF16)],
        compiler_params=_params("parallel", "parallel"), name="mla_proj",
    )(h, g0.reshape(1, D), w1, qg.reshape(1, -1), kvg.reshape(1, -1), wuqT, wuk, wuvT,
      tab, cos2.T, sin2.T)

    o = pl.pallas_call(
        functools.partial(_flash_body, tk, unroll), grid=(B, H, S // tq),
        in_specs=[pl.BlockSpec((1, 256, tq), lambda b, hd, i: (b, hd, i)),
                  pl.BlockSpec((1, 1, S, 256), lambda b, hd, i: (b, hd, 0, 0)),
                  pl.BlockSpec((1, MLA_V, S), lambda b, hd, i: (b, hd, 0))],
        out_specs=pl.BlockSpec((1, tq, MLA_V), lambda b, hd, i: (b, i, hd)),
        out_shape=jax.ShapeDtypeStruct((B, S, H * MLA_V), BF16),
        scratch_shapes=[pltpu.VMEM((2, tq // QSUB, tk, QSUB), F32)],
        compiler_params=_params("parallel", "parallel", "arbitrary"), name="mla_flash",
    )(qT, k, vT)
    return _out_proj(o, w_o.astype(BF16), g1, h, tm)


def _diff_flash_body(tk, unroll, lam_init, slope_ref, parts_ref, qT_ref, k_ref, vT_ref, lamp_ref, sg_ref,
                     o_ref, s_ref):
    S = k_ref.shape[1]
    d = DIFF_HD
    n = S // tk
    hd = pl.program_id(1)
    jd = pl.program_id(2)
    q0 = jd * tk
    slope = slope_ref[hd]
    s1, s2, s3 = parts_ref[3 * hd], parts_ref[3 * hd + 1], parts_ref[3 * hd + 2]

    qT = qT_ref[0]
    row = lax.broadcasted_iota(jnp.int32, (2 * d, QSUB), 0)
    zero = jnp.zeros_like(qT)
    qs = (jnp.where(row < d, qT, zero), jnp.where(row >= d, qT, zero))

    lane = lax.broadcasted_iota(jnp.int32, (tk, 2 * d), 1)
    rowf = lax.broadcasted_iota(jnp.int32, (tk, 2 * d), 0).astype(F32)
    k_aug = jnp.where(lane == 0, s1, jnp.where(lane == 1, s2, jnp.where(lane == 2, s3,
                      jnp.where(lane < 6, rowf, 0.0)))).astype(BF16)
    arow = lax.broadcasted_iota(jnp.int32, (16, QSUB), 0)
    colf = lax.broadcasted_iota(jnp.int32, (16, QSUB), 1).astype(F32)
    q_aug = jnp.where(arow < 3, -colf, jnp.where(arow == 3, s1, jnp.where(arow == 4, s2,
                      jnp.where(arow == 5, s3, 0.0))))
    q_pad = jnp.zeros((2 * d - 16, QSUB), BF16)

    def block(jj):
        jj = lax.convert_element_type(jj, jnp.int32)
        return jj + lax.convert_element_type(jj >= jd, jnp.int32)

    def scores(jj, slot):
        j = block(jj)
        kc = k_ref[0, pl.ds(pl.multiple_of(j * tk, tk), tk), :]
        kc = jnp.concatenate([kc, k_aug], axis=1)
        sgn = jnp.where(j < jd, 1.0, -1.0)
        aug = (q_aug * sgn).astype(BF16)
        for c in range(2):
            s_ref[slot, c] = _dot(kc, jnp.concatenate([qs[c], aug, q_pad], axis=0))

    def consume(jj, slot, stats):
        j = block(jj)
        vc = vT_ref[0, :, pl.ds(pl.multiple_of(j * tk, tk), tk)]
        kappa = slope * lax.convert_element_type(jnp.abs(q0 - j * tk), F32)
        return tuple(_online_update(s_ref[slot, c], vc, *stats[c], shift=kappa) for c in range(2))

    rel = (lax.broadcasted_iota(jnp.int32, (tk, QSUB), 1)
           - lax.broadcasted_iota(jnp.int32, (tk, QSUB), 0))
    bias_d = slope * jnp.abs(rel).astype(F32)
    kd = k_ref[0, pl.ds(pl.multiple_of(q0, tk), tk), :]
    vd = vT_ref[0, :, pl.ds(pl.multiple_of(q0, tk), tk)]
    sd = tuple(_dot(kd, q) for q in qs)
    m = n - 1
    if m > 0:
        scores(0, 0)
    stats = tuple(_online_update(s - bias_d, vd, *_flash_init(2 * d)) for s in sd)

    if m > 0:
        def pair(p, st):
            scores(2 * p + 1, 1)
            st = consume(2 * p, 0, st)
            scores(2 * p + 2, 0)
            return consume(2 * p + 1, 1, st)

        npair = (m - 1) // 2
        stats = lax.fori_loop(0, npair, pair, stats, unroll=max(1, min(unroll, npair)))
        if m % 2 == 0:
            scores(m - 1, 1)
            stats = consume(m - 2, 0, stats)
            stats = consume(m - 1, 1, stats)
        else:
            stats = consume(m - 1, 0, stats)

    lp = lamp_ref[...]
    lam = (jnp.exp(jnp.sum(lp[0:1] * lp[1:2], axis=1, keepdims=True))
           - jnp.exp(jnp.sum(lp[2:3] * lp[3:4], axis=1, keepdims=True)) + lam_init)
    gain = sg_ref[...] * (1.0 - lam_init)
    (_, l1, a1), (_, l2, a2) = stats
    oT = a1 / l1 - lam * (a2 / l2)
    oT = oT * lax.rsqrt(jnp.mean(oT * oT, axis=0, keepdims=True) + EPS) * gain
    o_ref[0] = oT.T.astype(o_ref.dtype)


def _diff_mixer(h, g0, w_qkv, lam_p, subln_g, w_o, g1, layer_idx, tm, tq, tk, unroll):
    B, S, D = h.shape
    H, d = DIFF_HEADS, DIFF_HD
    scale = d ** -0.5 * LOG2E
    wqT = w_qkv[:, :D].T.astype(BF16)
    wk = w_qkv[:, D:2 * D].astype(BF16)
    wvT = w_qkv[:, 2 * D:].T.astype(BF16)
    k, qT, vT = _norm_proj(h, g0, [(wk, None, 1.0, BF16)],
                           [(wqT, None, scale, BF16), (wvT, None, 1.0, BF16)], tm)
    lam_init = 0.8 - 0.6 * math.exp(-0.3 * layer_idx)
    slopes = (2.0 ** (-8.0 * np.arange(1, H + 1, dtype=np.float32) / H)) * np.float32(LOG2E)
    p1 = slopes.astype(BF16).astype(np.float32)
    p2 = (slopes - p1).astype(BF16).astype(np.float32)
    p3 = (slopes - p1 - p2).astype(BF16).astype(np.float32)
    assert np.all(p1 + p2 + p3 == slopes)
    parts = jnp.asarray(np.stack([p1, p2, p3], axis=1).reshape(3 * H))
    slopes = jnp.asarray(slopes)
    assert tq == tk == QSUB <= 256
    o = pl.pallas_call(
        functools.partial(_diff_flash_body, tk, unroll, lam_init), grid=(B, H, S // tq),
        in_specs=[pl.BlockSpec(memory_space=pltpu.SMEM), pl.BlockSpec(memory_space=pltpu.SMEM),
                  pl.BlockSpec((1, 2 * d, tq), lambda b, hd, i: (b, hd, i)),
                  pl.BlockSpec((1, S, 2 * d), lambda b, hd, i: (b, 0, hd)),
                  pl.BlockSpec((1, 2 * d, S), lambda b, hd, i: (b, hd, 0)),
                  pl.BlockSpec((4, d), lambda b, hd, i: (0, 0)),
                  pl.BlockSpec((2 * d, 1), lambda b, hd, i: (0, 0))],
        out_specs=pl.BlockSpec((1, tq, 2 * d), lambda b, hd, i: (b, i, hd)),
        out_shape=jax.ShapeDtypeStruct((B, S, H * 2 * d), BF16),
        scratch_shapes=[pltpu.VMEM((2, 2, tk, QSUB), F32)],
        compiler_params=_params("parallel", "parallel", "arbitrary"), name="diff_flash",
    )(slopes, parts, qT, k, vT, lam_p.astype(F32), subln_g.reshape(2 * d, 1).astype(F32))
    return _out_proj(o, w_o.astype(BF16), g1, h, tm)


def _mlstm_body(q_ref, kT_ref, v_ref, g_ref, gT_ref, o_ref, c_ref, m_ref):
    H, L, dv = MLSTM_HEADS, CHUNK, MLSTM_V
    sgn = 1 - 2 * pl.program_id(0)

    @pl.when(pl.program_id(2) == 0)
    def _():
        c_ref[...] = jnp.zeros_like(c_ref)
        m_ref[...] = jnp.zeros_like(m_ref)

    r = lax.broadcasted_iota(jnp.int32, (L, L), 0)
    c = lax.broadcasted_iota(jnp.int32, (L, L), 1)
    causal = (r - c) * sgn >= 0
    causal_t = (c - r) * sgn >= 0
    lane = lax.broadcasted_iota(jnp.int32, (L, dv), 1)
    ones_blk = jnp.where(lane == 0, 1.0, 0.0).astype(BF16)
    g = g_ref[0, 0]
    gT = gT_ref[0, 0]
    for hd in range(H):
        i_row = gT[hd:hd + 1, :]
        lf_row = _log_sigmoid(gT[H + hd:H + hd + 1, :])
        lf_col = _log_sigmoid(g[:, H + hd:H + hd + 1])
        bcum_col = jnp.sum(jnp.where(causal, lf_row, 0.0), axis=1, keepdims=True)
        bcum_row = jnp.sum(jnp.where(causal_t, lf_col, 0.0), axis=0, keepdims=True)
        b_last = jnp.sum(lf_row, axis=1, keepdims=True)
        m_prev = m_ref[hd][:, 0:1]

        dmat = jnp.where(causal, bcum_col - bcum_row + i_row, -jnp.inf)
        m_inter = bcum_col + m_prev
        m_t = jnp.maximum(jnp.max(dmat, axis=1, keepdims=True), m_inter)
        w_intra = jnp.exp(dmat - m_t)
        w_inter = jnp.exp(m_inter - m_t)

        qc = q_ref[0, :, hd * 128:(hd + 1) * 128]
        kT = kT_ref[0, hd * 128:(hd + 1) * 128, :]
        v_ext = jnp.concatenate([v_ref[0, :, hd * dv:(hd + 1) * dv], ones_blk], axis=1)
        c_ext = c_ref[hd]
        sqk = (_dot(qc, kT) * w_intra).astype(BF16)
        nd = w_inter * _dot(qc, c_ext.astype(BF16)) + _dot(sqk, v_ext)
        den = nd[:, dv:dv + 1]
        o_ref[0, 0, :, hd * dv:(hd + 1) * dv] = (
            nd[:, :dv] / jnp.maximum(jnp.abs(den), jnp.exp(-m_t)))

        g_s = b_last - bcum_row + i_row
        m_new = jnp.maximum(b_last + m_prev, jnp.max(g_s, axis=1, keepdims=True))
        w_s = jnp.exp(g_s - m_new)
        decay = jnp.exp(b_last + m_prev - m_new)
        c_ref[hd] = decay * c_ext + _dot((kT * w_s).astype(BF16), v_ext)
        m_ref[hd] = jnp.broadcast_to(m_new, m_ref.shape[1:])


def _mlstm_post_body(hf_ref, hb_ref, og_ref, ng_ref, w_ref, g_ref, h_ref, o_ref):
    H, dv = MLSTM_HEADS, MLSTM_V
    hs = hf_ref[0, 0] + hb_ref[0, 0]
    parts = []
    for hd in range(H):
        x = hs[:, hd * dv:(hd + 1) * dv]
        xc = x - jnp.mean(x, axis=1, keepdims=True)
        parts.append(xc * lax.rsqrt(jnp.mean(xc * xc, axis=1, keepdims=True) + EPS))
    y = jnp.concatenate(parts, axis=1) * ng_ref[...] * _sigmoid(og_ref[0].astype(F32))
    o_ref[0] = h_ref[0] + _rms(_dot(y.astype(BF16), w_ref[...]), g_ref[...])


def _mlstm_mixer(h, g0, w_in, b_gates, norm_g, w_out, g1, tm):
    B, S, D = h.shape
    H, dk, dv = MLSTM_HEADS, MLSTM_QK, MLSTM_V
    o1, o2, o3, o4 = H * dk, 2 * H * dk, 2 * H * dk + H * dv, 2 * H * dk + 2 * H * dv

    def pad_heads(w):
        w = w.reshape(D, H, dk)
        return jnp.concatenate([w, jnp.zeros((D, H, 128 - dk), w.dtype)], -1).reshape(D, H * 128)

    wq = pad_heads(w_in[:, :o1]).astype(BF16)
    wkT = pad_heads(w_in[:, o1:o2]).T.astype(BF16)
    wv = w_in[:, o2:o3].astype(BF16)
    wog = w_in[:, o3:o4].astype(BF16)
    wg = w_in[:, o4:].astype(BF16)
    q, v, og, g, kT, gT = _norm_proj(
        h, g0,
        [(wq, None, 1.0, BF16), (wv, None, 1.0, BF16), (wog, None, 1.0, BF16), (wg, b_gates, 1.0, F32)],
        [(wkT, None, dk ** -0.5, BF16), (wg.T, b_gates, 1.0, F32)], tm)
    g = jnp.moveaxis(g.reshape(B, S, 2, 2 * H), 2, 0)
    gT = jnp.moveaxis(gT.reshape(B, 2, 2 * H, S), 1, 0)
    NC = S // CHUNK

    def ck(dr, ci):
        return ci + dr * (NC - 1 - 2 * ci)

    hout = pl.pallas_call(
        _mlstm_body, grid=(2, B, NC),
        in_specs=[pl.BlockSpec((1, CHUNK, H * 128), lambda dr, b, ci: (b, ck(dr, ci), 0)),
                  pl.BlockSpec((1, H * 128, CHUNK), lambda dr, b, ci: (b, 0, ck(dr, ci))),
                  pl.BlockSpec((1, CHUNK, H * dv), lambda dr, b, ci: (b, ck(dr, ci), 0)),
                  pl.BlockSpec((1, 1, CHUNK, 2 * H), lambda dr, b, ci: (dr, b, ck(dr, ci), 0)),
                  pl.BlockSpec((1, 1, 2 * H, CHUNK), lambda dr, b, ci: (dr, b, 0, ck(dr, ci)))],
        out_specs=pl.BlockSpec((1, 1, CHUNK, H * dv), lambda dr, b, ci: (dr, b, ck(dr, ci), 0)),
        out_shape=jax.ShapeDtypeStruct((2, B, S, H * dv), F32),
        scratch_shapes=[pltpu.VMEM((H, 128, 2 * dv), F32), pltpu.VMEM((H, 1, 128), F32)],
        compiler_params=_params("parallel", "parallel", "arbitrary"), name="mlstm_scan",
    )(q, kT, v, g, gT)

    return pl.pallas_call(
        _mlstm_post_body, grid=(B, S // tm),
        in_specs=[pl.BlockSpec((1, 1, tm, H * dv), lambda b, i: (0, b, i, 0)),
                  pl.BlockSpec((1, 1, tm, H * dv), lambda b, i: (1, b, i, 0)),
                  pl.BlockSpec((1, tm, H * dv), lambda b, i: (b, i, 0)),
                  _resident((1, H * dv)), _resident((H * dv, D)), _resident((1, D)),
                  pl.BlockSpec((1, tm, D), lambda b, i: (b, i, 0))],
        out_specs=pl.BlockSpec((1, tm, D), lambda b, i: (b, i, 0)),
        out_shape=jax.ShapeDtypeStruct((B, S, D), F32),
        compiler_params=_params("parallel", "parallel"), name="mlstm_post",
    )(hout, hout, og, norm_g.reshape(1, -1), w_out.astype(BF16), g1.reshape(1, D), h)


def _ret_body(q_ref, kT_ref, v_ref, dl_ref, o_ref, r_ref):
    H, L, dk, dv = RET_HEADS, CHUNK, RET_QK, RET_V
    dr = pl.program_id(0)

    @pl.when(pl.program_id(2) == 0)
    def _():
        r_ref[...] = jnp.zeros_like(r_ref)

    r = lax.broadcasted_iota(jnp.int32, (L, L), 0)
    c = lax.broadcasted_iota(jnp.int32, (L, L), 1)
    rel = (r - c) * (1 - 2 * dr)
    relf = jnp.maximum(rel, 0).astype(F32)
    t_col = lax.broadcasted_iota(jnp.int32, (L, 1), 0)
    s_row = lax.broadcasted_iota(jnp.int32, (1, L), 1)
    xi_pow = (t_col + 1 + dr * (L - 1 - 2 * t_col)).astype(F32)
    zeta_pow = (L - 1 - s_row + dr * (2 * s_row - (L - 1))).astype(F32)
    lg = _log_sigmoid(dl_ref[0])
    for hd in range(H):
        lgh = lg[:, hd:hd + 1]
        dmask = jnp.where(rel >= 0, jnp.exp(relf * lgh), 0.0)
        qc = q_ref[0, :, hd * dk:(hd + 1) * dk]
        kT = kT_ref[0, hd * dk:(hd + 1) * dk, :]
        vc = v_ref[0, :, hd * dv:(hd + 1) * dv]
        state = r_ref[hd]
        inner = _dot((_dot(qc, kT) * dmask).astype(BF16), vc)
        cross = _dot(qc, state.astype(BF16)) * jnp.exp(xi_pow * lgh)
        o_ref[0, 0, :, hd * dv:(hd + 1) * dv] = inner + cross
        kz = (kT * jnp.exp(zeta_pow * lgh)).astype(BF16)
        r_ref[hd] = jnp.exp(L * lgh) * state + _dot(kz, vc)


def _ret_post_body(yf_ref, yb_ref, gate_ref, ng_ref, w_ref, g_ref, h_ref, o_ref):
    H, dv = RET_HEADS, RET_V
    ys = yf_ref[0, 0] + yb_ref[0, 0]
    parts = []
    for hd in range(H):
        x = ys[:, hd * dv:(hd + 1) * dv]
        xc = x - jnp.mean(x, axis=1, keepdims=True)
        parts.append(xc * lax.rsqrt(jnp.mean(xc * xc, axis=1, keepdims=True) + EPS))
    gate = gate_ref[0].astype(F32)
    y = jnp.concatenate(parts, axis=1) * ng_ref[...] * (gate * _sigmoid(gate))
    o_ref[0] = h_ref[0] + _rms(_dot(y.astype(BF16), w_ref[...]), g_ref[...])


def _ret_mixer(h, g0, w_in, decay_logit, norm_g, w_o, g1, tm):
    B, S, D = h.shape
    H, dk, dv = RET_HEADS, RET_QK, RET_V
    o1, o2, o3 = H * dk, 2 * H * dk, 2 * H * dk + H * dv
    wq = w_in[:, :o1].astype(BF16)
    wkT = w_in[:, o1:o2].T.astype(BF16)
    wv = w_in[:, o2:o3].astype(BF16)
    wgate = w_in[:, o3:].astype(BF16)
    q, v, gate, kT = _norm_proj(
        h, g0, [(wq, None, 1.0, BF16), (wv, None, 1.0, BF16), (wgate, None, 1.0, BF16)],
        [(wkT, None, dk ** -0.5, BF16)], tm)
    NC = S // CHUNK

    def ck(dr, ci):
        return ci + dr * (NC - 1 - 2 * ci)

    y = pl.pallas_call(
        _ret_body, grid=(2, B, NC),
        in_specs=[pl.BlockSpec((1, CHUNK, H * dk), lambda dr, b, ci: (b, ck(dr, ci), 0)),
                  pl.BlockSpec((1, H * dk, CHUNK), lambda dr, b, ci: (b, 0, ck(dr, ci))),
                  pl.BlockSpec((1, CHUNK, H * dv), lambda dr, b, ci: (b, ck(dr, ci), 0)),
                  pl.BlockSpec((1, 1, H), lambda dr, b, ci: (dr, 0, 0))],
        out_specs=pl.BlockSpec((1, 1, CHUNK, H * dv), lambda dr, b, ci: (dr, b, ck(dr, ci), 0)),
        out_shape=jax.ShapeDtypeStruct((2, B, S, H * dv), F32),
        scratch_shapes=[pltpu.VMEM((H, dk, dv), F32)],
        compiler_params=_params("parallel", "parallel", "arbitrary"), name="ret_scan",
    )(q, kT, v, decay_logit.astype(F32).reshape(2, 1, H))

    return pl.pallas_call(
        _ret_post_body, grid=(B, S // tm),
        in_specs=[pl.BlockSpec((1, 1, tm, H * dv), lambda b, i: (0, b, i, 0)),
                  pl.BlockSpec((1, 1, tm, H * dv), lambda b, i: (1, b, i, 0)),
                  pl.BlockSpec((1, tm, H * dv), lambda b, i: (b, i, 0)),
                  _resident((1, H * dv)), _resident((H * dv, D)), _resident((1, D)),
                  pl.BlockSpec((1, tm, D), lambda b, i: (b, i, 0))],
        out_specs=pl.BlockSpec((1, tm, D), lambda b, i: (b, i, 0)),
        out_shape=jax.ShapeDtypeStruct((B, S, D), F32),
        compiler_params=_params("parallel", "parallel"), name="ret_post",
    )(y, y, gate, norm_g.reshape(1, -1), w_o.astype(BF16), g1.reshape(1, D), h)


def _tile(s, want):
    return min(want, s)


def kernel(x, norm_g, ffn_w_up, ffn_conv_w, ffn_conv_b, ffn_w_down, mla_w_dq, mla_q_norm_g, mla_w_uq, mla_w_dkv, mla_kv_norm_g, mla_w_ukv, mla_w_o, diff_w_qkv, diff_lambda, diff_subln_g, diff_w_o, mlstm_w_in, mlstm_b_gates, mlstm_norm_g, mlstm_w_out, ret_w_in, ret_decay_logit, ret_norm_g, ret_w_o):
    depth = norm_g.shape[0]
    S = x.shape[1]
    tm = _tile(S, 512)
    mla_t = (_tile(S, MLA_TQ), _tile(S, MLA_TK), MLA_UNROLL)
    diff_t = (_tile(S, DIFF_TQ), _tile(S, DIFF_TK), DIFF_UNROLL)
    h = x
    for i in range(depth):
        kind, j = i % 4, i // 4
        g0, g1, g2, g3 = norm_g[i, 0], norm_g[i, 1], norm_g[i, 2], norm_g[i, 3]
        if kind == 0:
            h = _mla_mixer(h, g0, mla_w_dq[j], mla_q_norm_g[j], mla_w_uq[j], mla_w_dkv[j],
                           mla_kv_norm_g[j], mla_w_ukv[j], mla_w_o[j], g1, tm, *mla_t)
        elif kind == 1:
            h = _diff_mixer(h, g0, diff_w_qkv[j], diff_lambda[j], diff_subln_g[j], diff_w_o[j],
                            g1, i, tm, *diff_t)
        elif kind == 2:
            h = _mlstm_mixer(h, g0, mlstm_w_in[j], mlstm_b_gates[j], mlstm_norm_g[j],
                             mlstm_w_out[j], g1, tm)
        else:
            h = _ret_mixer(h, g0, ret_w_in[j], ret_decay_logit[j], ret_norm_g[j], ret_w_o[j], g1, tm)
        h = _conv_ffn(h, g2, ffn_w_up[i].astype(BF16), ffn_conv_w[i], ffn_conv_b[i],
                      ffn_w_down[i].astype(BF16), g3, tm)
    return h
```

```python
import functools
import math

import jax
import jax.numpy as jnp
import numpy as np
from jax import lax
from jax.experimental import pallas as pl
from jax.experimental.pallas import tpu as pltpu

F32 = jnp.float32
BF16 = jnp.bfloat16

EPS = 1e-6
LOG2E = 1.4426950408889634
CHUNK = 128
HALO = 8

MLA_HEADS, MLA_NOPE, MLA_ROPE, MLA_V = 8, 128, 64, 128
MLA_Q_LORA, MLA_KV_LORA = 384, 256
ROPE_THETA = 10000.0
DIFF_HEADS, DIFF_HD = 8, 64
MLSTM_HEADS, MLSTM_QK, MLSTM_V = 8, 64, 128
RET_HEADS, RET_QK, RET_V = 4, 256, 512
CONV_W = 3

VMEM_LIMIT = 56 * 1024 * 1024


def _params(*sem):
    return pltpu.CompilerParams(dimension_semantics=sem, vmem_limit_bytes=VMEM_LIMIT)


def _resident(shape):
    nd = len(shape)
    return pl.BlockSpec(shape, lambda *_: (0,) * nd, pipeline_mode=pl.Buffered(1))


def _rms(x, g):
    return x * lax.rsqrt(jnp.mean(x * x, axis=-1, keepdims=True) + EPS) * g


def _dot(a, b):
    return jnp.dot(a, b, preferred_element_type=F32)


def _dot_nt(a, b):
    return lax.dot_general(a, b, (((1,), (1,)), ((), ())), preferred_element_type=F32)


def _log_sigmoid(x):
    return jnp.minimum(x, 0.0) - jnp.log1p(jnp.exp(-jnp.abs(x)))


def _sigmoid(x):
    return 1.0 / (1.0 + jnp.exp(-x))


COL_CHUNK = 512


def _proj_body(n_nat, n_t, nat_scales, t_scales, *refs):
    h_ref, g_ref = refs[0], refs[1]
    nat_w = refs[2:2 + n_nat]
    nat_b = refs[2 + n_nat:2 + 2 * n_nat]
    t_w = refs[2 + 2 * n_nat:2 + 2 * n_nat + n_t]
    t_b = refs[2 + 2 * n_nat + n_t:2 + 2 * n_nat + 2 * n_t]
    outs = refs[2 + 2 * n_nat + 2 * n_t:]
    nat_o, t_o = outs[:n_nat], outs[n_nat:]

    xn = _rms(h_ref[0], g_ref[...]).astype(BF16)
    for w_ref, b_ref, o_ref, sc in zip(nat_w, nat_b, nat_o, nat_scales):
        m = w_ref.shape[1]
        for c0 in range(0, m, COL_CHUNK):
            cw = min(COL_CHUNK, m - c0)
            y = _dot(xn, w_ref[:, c0:c0 + cw]) + b_ref[:, c0:c0 + cw]
            if sc != 1.0:
                y = y * sc
            o_ref[0, :, c0:c0 + cw] = y.astype(o_ref.dtype)
    for w_ref, b_ref, o_ref, sc in zip(t_w, t_b, t_o, t_scales):
        m = w_ref.shape[0]
        for r0 in range(0, m, COL_CHUNK):
            rw = min(COL_CHUNK, m - r0)
            y = _dot_nt(w_ref[r0:r0 + rw, :], xn) + b_ref[r0:r0 + rw, :]
            if sc != 1.0:
                y = y * sc
            o_ref[0, r0:r0 + rw, :] = y.astype(o_ref.dtype)


def _norm_proj(h, g, nat, tr, tm):
    B, S, D = h.shape
    args = [h, g.reshape(1, D)]
    specs = [pl.BlockSpec((1, tm, D), lambda b, i: (b, i, 0)), _resident((1, D))]
    for w, _, _, _ in nat:
        args.append(w)
        specs.append(_resident(w.shape))
    for w, bias, _, _ in nat:
        m = w.shape[1]
        args.append(jnp.zeros((1, m), F32) if bias is None else bias.reshape(1, m).astype(F32))
        specs.append(_resident((1, m)))
    for w, _, _, _ in tr:
        args.append(w)
        specs.append(_resident(w.shape))
    for w, bias, _, _ in tr:
        m = w.shape[0]
        args.append(jnp.zeros((m, 1), F32) if bias is None else bias.reshape(m, 1).astype(F32))
        specs.append(_resident((m, 1)))
    out_shape, out_specs = [], []
    for w, _, _, dt in nat:
        m = w.shape[1]
        out_shape.append(jax.ShapeDtypeStruct((B, S, m), dt))
        out_specs.append(pl.BlockSpec((1, tm, m), lambda b, i: (b, i, 0)))
    for w, _, _, dt in tr:
        m = w.shape[0]
        out_shape.append(jax.ShapeDtypeStruct((B, m, S), dt))
        out_specs.append(pl.BlockSpec((1, m, tm), lambda b, i: (b, 0, i)))
    body = functools.partial(_proj_body, len(nat), len(tr),
                             tuple(x[2] for x in nat), tuple(x[2] for x in tr))
    return pl.pallas_call(
        body, grid=(B, S // tm), in_specs=specs, out_specs=out_specs, out_shape=out_shape,
        compiler_params=_params("parallel", "parallel"), name="norm_proj")(*args)


def _out_body(a_ref, w_ref, g_ref, h_ref, o_ref):
    y = _dot(a_ref[0], w_ref[...])
    o_ref[0] = h_ref[0] + _rms(y, g_ref[...])


def _out_proj(a, w, g, h, tm):
    B, S, K = a.shape
    D = h.shape[-1]
    return pl.pallas_call(
        _out_body, grid=(B, S // tm),
        in_specs=[pl.BlockSpec((1, tm, K), lambda b, i: (b, i, 0)), _resident(w.shape),
                  _resident((1, D)), pl.BlockSpec((1, tm, D), lambda b, i: (b, i, 0))],
        out_specs=pl.BlockSpec((1, tm, D), lambda b, i: (b, i, 0)),
        out_shape=jax.ShapeDtypeStruct((B, S, D), F32),
        compiler_params=_params("parallel", "parallel"), name="out_proj")(a, w, g.reshape(1, D), h)


FFN_CHUNK = 256


def _ffn_body(tm, F, hp_ref, hm_ref, hn_ref, g2_ref, wup_ref, cw_ref, cb_ref, wdn_ref, g3_ref,
              o_ref, acc_ref, u_ref):
    i = pl.program_id(1)
    last = pl.num_programs(1) - 1
    g2 = g2_ref[...]
    hm = hm_ref[0]
    xp = _rms(hp_ref[0], g2) * jnp.where(i == 0, 0.0, 1.0)
    xn_ = _rms(hn_ref[0], g2) * jnp.where(i == last, 0.0, 1.0)
    x_ext = jnp.concatenate([xp, _rms(hm, g2), xn_], axis=0).astype(BF16)

    acc_ref[...] = jnp.zeros_like(acc_ref)

    n = F // FFN_CHUNK

    def up(c, slot):
        c0 = pl.multiple_of(c * FFN_CHUNK, FFN_CHUNK)
        for half, off in enumerate((0, F)):
            u_ref[slot, half] = _dot(x_ext, wup_ref[:, pl.ds(off + c0, FFN_CHUNK)])

    def down(c, slot):
        c0 = pl.multiple_of(c * FFN_CHUNK, FFN_CHUNK)
        halves = []
        for half, off in enumerate((0, F)):
            w = cw_ref[:, pl.ds(off + c0, FFN_CHUNK)]
            b = cb_ref[:, pl.ds(off + c0, FFN_CHUNK)]
            halves.append(b + w[0:1] * u_ref[slot, half, HALO - 1:HALO - 1 + tm, :]
                          + w[1:2] * u_ref[slot, half, HALO:HALO + tm, :]
                          + w[2:3] * u_ref[slot, half, HALO + 1:HALO + 1 + tm, :])
        hid = (jax.nn.gelu(halves[0]) * halves[1]).astype(BF16)
        acc_ref[...] += _dot(hid, wdn_ref[pl.ds(c0, FFN_CHUNK), :])

    def pair(p, carry):
        up(2 * p + 1, 1)
        down(2 * p, 0)
        up(2 * p + 2, 0)
        down(2 * p + 1, 1)
        return carry

    up(0, 0)
    lax.fori_loop(0, (n - 1) // 2, pair, 0)
    down(n - 1, 0)
    o_ref[0] = hm + _rms(acc_ref[...], g3_ref[...])


def _conv_ffn(h, g2, w_up, conv_w, conv_b, w_down, g3, tm):
    B, S, D = h.shape
    F = w_down.shape[0]
    assert F % FFN_CHUNK == 0 and (F // FFN_CHUNK) % 2 == 1
    nb = tm // HALO
    body = functools.partial(_ffn_body, tm, F)
    return pl.pallas_call(
        body, grid=(B, S // tm),
        in_specs=[
            pl.BlockSpec((1, HALO, D), lambda b, i: (b, jnp.maximum(i * nb - 1, 0), 0)),
            pl.BlockSpec((1, tm, D), lambda b, i: (b, i, 0)),
            pl.BlockSpec((1, HALO, D), lambda b, i: (b, jnp.minimum((i + 1) * nb, S // HALO - 1), 0)),
            _resident((1, D)), _resident(w_up.shape), _resident(conv_w.shape),
            _resident((1, 2 * F)), _resident(w_down.shape), _resident((1, D)),
        ],
        out_specs=pl.BlockSpec((1, tm, D), lambda b, i: (b, i, 0)),
        out_shape=jax.ShapeDtypeStruct((B, S, D), F32),
        scratch_shapes=[pltpu.VMEM((tm, D), F32),
                        pltpu.VMEM((2, 2, tm + 2 * HALO, FFN_CHUNK), F32)],
        compiler_params=_params("parallel", "parallel"), name="conv_ffn",
    )(h, h, h, g2.reshape(1, D), w_up, conv_w, conv_b.reshape(1, 2 * F), w_down, g3.reshape(1, D))


def _mla_proj_body(h_ref, g_ref, w1_ref, qg_ref, kvg_ref, wuqT_ref, wuk_ref, wuvT_ref,
                   tab_ref, cosT_ref, sinT_ref, qT_ref, k_ref, vT_ref):
    H = MLA_HEADS
    tm = h_ref.shape[1]
    a = _rms(h_ref[0], g_ref[...]).astype(BF16)
    t1 = _dot(a, w1_ref[...])
    cq = _rms(t1[:, :MLA_Q_LORA], qg_ref[...]).astype(BF16)
    ckv = _rms(t1[:, MLA_Q_LORA:MLA_Q_LORA + MLA_KV_LORA], kvg_ref[...]).astype(BF16)
    z = t1[:, MLA_Q_LORA + MLA_KV_LORA:] * tab_ref[...]
    lane = lax.broadcasted_iota(jnp.int32, z.shape, 1)
    kr = jnp.where(lane < MLA_ROPE, z + pltpu.roll(z, MLA_ROPE, 1), 0.0).astype(BF16)
    k_nope = _dot(ckv, wuk_ref[...]).astype(BF16)
    for hd in range(H):
        k_ref[0, hd] = jnp.concatenate([k_nope[:, hd * MLA_NOPE:(hd + 1) * MLA_NOPE], kr], axis=1)
    vT_ref[0] = _dot_nt(wuvT_ref[...], ckv).astype(BF16)
    scale = (MLA_NOPE + MLA_ROPE) ** -0.5 * LOG2E
    cosT, sinT = cosT_ref[...], sinT_ref[...]
    zeros = jnp.zeros((256 - MLA_NOPE - MLA_ROPE, tm), F32)
    for hd in range(H):
        blk = _dot_nt(wuqT_ref[hd * 256:(hd + 1) * 256, :], cq)
        r = blk[128:192] * cosT + blk[192:256] * sinT
        qT_ref[0, hd * 256:(hd + 1) * 256, :] = (
            jnp.concatenate([blk[:128], r, zeros], axis=0) * scale).astype(BF16)


QSUB = 256
MLA_TQ, MLA_TK, MLA_UNROLL = 1024, 1024, 1
DIFF_TQ, DIFF_TK, DIFF_UNROLL = 256, 256, 15


def _online_update(s, vc, m, l, acc, shift=None):
    s_max = jnp.max(s, axis=0, keepdims=True)
    if shift is None:
        m_new = jnp.maximum(m, s_max)
        p = jnp.exp2(s - m_new)
    else:
        m_new = jnp.maximum(m, s_max - shift)
        p = jnp.exp2(s - (m_new + shift))
    alpha = jnp.exp2(m - m_new)
    return (m_new, alpha * l + jnp.sum(p, axis=0, keepdims=True),
            alpha * acc + _dot(vc, p.astype(BF16)))


def _flash_init(dv):
    return (jnp.full((1, QSUB), -jnp.inf, F32), jnp.zeros((1, QSUB), F32), jnp.zeros((dv, QSUB), F32))


def _flash_body(tk, unroll, qT_ref, k_ref, vT_ref, o_ref, s_ref):
    S = k_ref.shape[2]
    G = qT_ref.shape[2] // QSUB
    dv = vT_ref.shape[1]
    n = S // tk
    qs = [qT_ref[0, :, g * QSUB:(g + 1) * QSUB] for g in range(G)]

    def scores(j, slot):
        kc = k_ref[0, 0, pl.ds(pl.multiple_of(j * tk, tk), tk), :]
        for g in range(G):
            s_ref[slot, g] = _dot(kc, qs[g])

    def consume(j, slot, stats):
        vc = vT_ref[0, :, pl.ds(pl.multiple_of(j * tk, tk), tk)]
        return tuple(_online_update(s_ref[slot, g], vc, *stats[g]) for g in range(G))

    def pair(p, stats):
        scores(2 * p + 1, 1)
        stats = consume(2 * p, 0, stats)
        scores(2 * p + 2, 0)
        return consume(2 * p + 1, 1, stats)

    stats = tuple(_flash_init(dv) for _ in range(G))
    scores(0, 0)
    npair = (n - 1) // 2
    stats = lax.fori_loop(0, npair, pair, stats, unroll=max(1, min(unroll, npair)))
    if n % 2 == 0:
        scores(n - 1, 1)
        stats = consume(n - 2, 0, stats)
        res = consume(n - 1, 1, stats)
    else:
        res = consume(n - 1, 0, stats)
    for g, (_, l, acc) in enumerate(res):
        o_ref[0, g * QSUB:(g + 1) * QSUB, :] = (acc / l).T.astype(o_ref.dtype)


def _mla_mixer(h, g0, w_dq, qg, w_uq, w_dkv, kvg, w_ukv, w_o, g1, tm, tq, tk, unroll):
    B, S, D = h.shape
    H = MLA_HEADS
    half = MLA_ROPE // 2

    def rot_cols(w):
        return jnp.concatenate([-w[..., half:], w[..., :half]], axis=-1)

    wkr = w_dkv[:, MLA_KV_LORA:]
    w1 = jnp.concatenate([w_dq, w_dkv[:, :MLA_KV_LORA], wkr, rot_cols(wkr)], axis=1).astype(BF16)
    wq = w_uq.reshape(MLA_Q_LORA, H, MLA_NOPE + MLA_ROPE)
    wq_r = wq[..., MLA_NOPE:]
    wuqT = jnp.concatenate([wq[..., :MLA_NOPE], wq_r, rot_cols(wq_r)], axis=-1)
    wuqT = wuqT.reshape(MLA_Q_LORA, H * 256).T.astype(BF16)
    wkv = w_ukv.reshape(MLA_KV_LORA, H, MLA_NOPE + MLA_V)
    wuk = wkv[..., :MLA_NOPE].reshape(MLA_KV_LORA, H * MLA_NOPE).astype(BF16)
    wuvT = wkv[..., MLA_NOPE:].reshape(MLA_KV_LORA, H * MLA_V).T.astype(BF16)

    inv = ROPE_THETA ** (-jnp.arange(0, MLA_ROPE, 2, dtype=F32) / MLA_ROPE)
    ang = jnp.arange(S, dtype=F32)[:, None] * inv[None, :]
    cos, sin = jnp.cos(ang), jnp.sin(ang)
    cos2, sin2 = jnp.concatenate([cos, cos], 1), jnp.concatenate([sin, sin], 1)
    tab = jnp.concatenate([cos2, sin2], 1)

    qT, k, vT = pl.pallas_call(
        _mla_proj_body, grid=(B, S // tm),
        in_specs=[pl.BlockSpec((1, tm, D), lambda b, i: (b, i, 0)), _resident((1, D)),
                  _resident(w1.shape), _resident((1, MLA_Q_LORA)), _resident((1, MLA_KV_LORA)),
                  _resident(wuqT.shape), _resident(wuk.shape), _resident(wuvT.shape),
                  pl.BlockSpec((tm, 128), lambda b, i: (i, 0)),
                  pl.BlockSpec((MLA_ROPE, tm), lambda b, i: (0, i)),
                  pl.BlockSpec((MLA_ROPE, tm), lambda b, i: (0, i))],
        out_specs=[pl.BlockSpec((1, H * 256, tm), lambda b, i: (b, 0, i)),
                   pl.BlockSpec((1, H, tm, 256), lambda b, i: (b, 0, i, 0)),
                   pl.BlockSpec((1, H * MLA_V, tm), lambda b, i: (b, 0, i))],
        out_shape=[jax.ShapeDtypeStruct((B, H * 256, S), BF16),
                   jax.ShapeDtypeStruct((B, H, S, 256), BF16),
                   jax.ShapeDtypeStruct((B, H * MLA_V, S), BF16)],
        compiler_params=_params("parallel", "parallel"), name="mla_proj",
    )(h, g0.reshape(1, D), w1, qg.reshape(1, -1), kvg.reshape(1, -1), wuqT, wuk, wuvT,
      tab, cos2.T, sin2.T)

    o = pl.pallas_call(
        functools.partial(_flash_body, tk, unroll), grid=(B, H, S // tq),
        in_specs=[pl.BlockSpec((1, 256, tq), lambda b, hd, i: (b, hd, i)),
                  pl.BlockSpec((1, 1, S, 256), lambda b, hd, i: (b, hd, 0, 0)),
                  pl.BlockSpec((1, MLA_V, S), lambda b, hd, i: (b, hd, 0))],
        out_specs=pl.BlockSpec((1, tq, MLA_V), lambda b, hd, i: (b, i, hd)),
        out_shape=jax.ShapeDtypeStruct((B, S, H * MLA_V), BF16),
        scratch_shapes=[pltpu.VMEM((2, tq // QSUB, tk, QSUB), F32)],
        compiler_params=_params("parallel", "parallel", "arbitrary"), name="mla_flash",
    )(qT, k, vT)
    return _out_proj(o, w_o.astype(BF16), g1, h, tm)


def _diff_flash_body(tk, unroll, lam_init, slope_ref, parts_ref, qT_ref, k_ref, vT_ref, lamp_ref, sg_ref,
                     o_ref, s_ref):
    S = k_ref.shape[1]
    d = DIFF_HD
    n = S // tk
    hd = pl.program_id(1)
    jd = pl.program_id(2)
    q0 = jd * tk
    slope = slope_ref[hd]
    s1, s2, s3 = parts_ref[3 * hd], parts_ref[3 * hd + 1], parts_ref[3 * hd + 2]

    qT = qT_ref[0]
    row = lax.broadcasted_iota(jnp.int32, (2 * d, QSUB), 0)
    zero = jnp.zeros_like(qT)
    qs = (jnp.where(row < d, qT, zero), jnp.where(row >= d, qT, zero))

    lane = lax.broadcasted_iota(jnp.int32, (tk, 2 * d), 1)
    rowf = lax.broadcasted_iota(jnp.int32, (tk, 2 * d), 0).astype(F32)
    k_aug = jnp.where(lane == 0, s1, jnp.where(lane == 1, s2, jnp.where(lane == 2, s3,
                      jnp.where(lane < 6, rowf, 0.0)))).astype(BF16)
    arow = lax.broadcasted_iota(jnp.int32, (16, QSUB), 0)
    colf = lax.broadcasted_iota(jnp.int32, (16, QSUB), 1).astype(F32)
    q_aug = jnp.where(arow < 3, -colf, jnp.where(arow == 3, s1, jnp.where(arow == 4, s2,
                      jnp.where(arow == 5, s3, 0.0))))
    q_pad = jnp.zeros((2 * d - 16, QSUB), BF16)

    def block(jj):
        jj = lax.convert_element_type(jj, jnp.int32)
        return jj + lax.convert_element_type(jj >= jd, jnp.int32)

    def scores(jj, slot):
        j = block(jj)
        kc = k_ref[0, pl.ds(pl.multiple_of(j * tk, tk), tk), :]
        kc = jnp.concatenate([kc, k_aug], axis=1)
        sgn = jnp.where(j < jd, 1.0, -1.0)
        aug = (q_aug * sgn).astype(BF16)
        for c in range(2):
            s_ref[slot, c] = _dot(kc, jnp.concatenate([qs[c], aug, q_pad], axis=0))

    def consume(jj, slot, stats):
        j = block(jj)
        vc = vT_ref[0, :, pl.ds(pl.multiple_of(j * tk, tk), tk)]
        kappa = slope * lax.convert_element_type(jnp.abs(q0 - j * tk), F32)
        return tuple(_online_update(s_ref[slot, c], vc, *stats[c], shift=kappa) for c in range(2))

    rel = (lax.broadcasted_iota(jnp.int32, (tk, QSUB), 1)
           - lax.broadcasted_iota(jnp.int32, (tk, QSUB), 0))
    bias_d = slope * jnp.abs(rel).astype(F32)
    kd = k_ref[0, pl.ds(pl.multiple_of(q0, tk), tk), :]
    vd = vT_ref[0, :, pl.ds(pl.multiple_of(q0, tk), tk)]
    sd = tuple(_dot(kd, q) for q in qs)
    m = n - 1
    if m > 0:
        scores(0, 0)
    stats = tuple(_online_update(s - bias_d, vd, *_flash_init(2 * d)) for s in sd)

    if m > 0:
        def pair(p, st):
            scores(2 * p + 1, 1)
            st = consume(2 * p, 0, st)
            scores(2 * p + 2, 0)
            return consume(2 * p + 1, 1, st)

        npair = (m - 1) // 2
        stats = lax.fori_loop(0, npair, pair, stats, unroll=max(1, min(unroll, npair)))
        if m % 2 == 0:
            scores(m - 1, 1)
            stats = consume(m - 2, 0, stats)
            stats = consume(m - 1, 1, stats)
        else:
            stats = consume(m - 1, 0, stats)

    lp = lamp_ref[...]
    lam = (jnp.exp(jnp.sum(lp[0:1] * lp[1:2], axis=1, keepdims=True))
           - jnp.exp(jnp.sum(lp[2:3] * lp[3:4], axis=1, keepdims=True)) + lam_init)
    gain = sg_ref[...] * (1.0 - lam_init)
    (_, l1, a1), (_, l2, a2) = stats
    oT = a1 / l1 - lam * (a2 / l2)
    oT = oT * lax.rsqrt(jnp.mean(oT * oT, axis=0, keepdims=True) + EPS) * gain
    o_ref[0] = oT.T.astype(o_ref.dtype)


def _diff_mixer(h, g0, w_qkv, lam_p, subln_g, w_o, g1, layer_idx, tm, tq, tk, unroll):
    B, S, D = h.shape
    H, d = DIFF_HEADS, DIFF_HD
    scale = d ** -0.5 * LOG2E
    wqT = w_qkv[:, :D].T.astype(BF16)
    wk = w_qkv[:, D:2 * D].astype(BF16)
    wvT = w_qkv[:, 2 * D:].T.astype(BF16)
    k, qT, vT = _norm_proj(h, g0, [(wk, None, 1.0, BF16)],
                           [(wqT, None, scale, BF16), (wvT, None, 1.0, BF16)], tm)
    lam_init = 0.8 - 0.6 * math.exp(-0.3 * layer_idx)
    slopes = (2.0 ** (-8.0 * np.arange(1, H + 1, dtype=np.float32) / H)) * np.float32(LOG2E)
    p1 = slopes.astype(BF16).astype(np.float32)
    p2 = (slopes - p1).astype(BF16).astype(np.float32)
    p3 = (slopes - p1 - p2).astype(BF16).astype(np.float32)
    assert np.all(p1 + p2 + p3 == slopes)
    parts = jnp.asarray(np.stack([p1, p2, p3], axis=1).reshape(3 * H))
    slopes = jnp.asarray(slopes)
    assert tq == tk == QSUB <= 256
    o = pl.pallas_call(
        functools.partial(_diff_flash_body, tk, unroll, lam_init), grid=(B, H, S // tq),
        in_specs=[pl.BlockSpec(memory_space=pltpu.SMEM), pl.BlockSpec(memory_space=pltpu.SMEM),
                  pl.BlockSpec((1, 2 * d, tq), lambda b, hd, i: (b, hd, i)),
                  pl.BlockSpec((1, S, 2 * d), lambda b, hd, i: (b, 0, hd)),
                  pl.BlockSpec((1, 2 * d, S), lambda b, hd, i: (b, hd, 0)),
                  pl.BlockSpec((4, d), lambda b, hd, i: (0, 0)),
                  pl.BlockSpec((2 * d, 1), lambda b, hd, i: (0, 0))],
        out_specs=pl.BlockSpec((1, tq, 2 * d), lambda b, hd, i: (b, i, hd)),
        out_shape=jax.ShapeDtypeStruct((B, S, H * 2 * d), BF16),
        scratch_shapes=[pltpu.VMEM((2, 2, tk, QSUB), F32)],
        compiler_params=_params("parallel", "parallel", "arbitrary"), name="diff_flash",
    )(slopes, parts, qT, k, vT, lam_p.astype(F32), subln_g.reshape(2 * d, 1).astype(F32))
    return _out_proj(o, w_o.astype(BF16), g1, h, tm)


def _mlstm_body(q_ref, kT_ref, v_ref, g_ref, gT_ref, o_ref, c_ref, m_ref):
    H, L, dv = MLSTM_HEADS, CHUNK, MLSTM_V
    sgn = 1 - 2 * pl.program_id(0)

    @pl.when(pl.program_id(2) == 0)
    def _():
        c_ref[...] = jnp.zeros_like(c_ref)
        m_ref[...] = jnp.zeros_like(m_ref)

    r = lax.broadcasted_iota(jnp.int32, (L, L), 0)
    c = lax.broadcasted_iota(jnp.int32, (L, L), 1)
    causal = (r - c) * sgn >= 0
    causal_t = (c - r) * sgn >= 0
    lane = lax.broadcasted_iota(jnp.int32, (L, dv), 1)
    ones_blk = jnp.where(lane == 0, 1.0, 0.0).astype(BF16)
    g = g_ref[0, 0]
    gT = gT_ref[0, 0]
    for hd in range(H):
        i_row = gT[hd:hd + 1, :]
        lf_row = _log_sigmoid(gT[H + hd:H + hd + 1, :])
        lf_col = _log_sigmoid(g[:, H + hd:H + hd + 1])
        bcum_col = jnp.sum(jnp.where(causal, lf_row, 0.0), axis=1, keepdims=True)
        bcum_row = jnp.sum(jnp.where(causal_t, lf_col, 0.0), axis=0, keepdims=True)
        b_last = jnp.sum(lf_row, axis=1, keepdims=True)
        m_prev = m_ref[hd][:, 0:1]

        dmat = jnp.where(causal, bcum_col - bcum_row + i_row, -jnp.inf)
        m_inter = bcum_col + m_prev
        m_t = jnp.maximum(jnp.max(dmat, axis=1, keepdims=True), m_inter)
        w_intra = jnp.exp(dmat - m_t)
        w_inter = jnp.exp(m_inter - m_t)

        qc = q_ref[0, :, hd * 128:(hd + 1) * 128]
        kT = kT_ref[0, hd * 128:(hd + 1) * 128, :]
        v_ext = jnp.concatenate([v_ref[0, :, hd * dv:(hd + 1) * dv], ones_blk], axis=1)
        c_ext = c_ref[hd]
        sqk = (_dot(qc, kT) * w_intra).astype(BF16)
        nd = w_inter * _dot(qc, c_ext.astype(BF16)) + _dot(sqk, v_ext)
        den = nd[:, dv:dv + 1]
        o_ref[0, 0, :, hd * dv:(hd + 1) * dv] = (
            nd[:, :dv] / jnp.maximum(jnp.abs(den), jnp.exp(-m_t)))

        g_s = b_last - bcum_row + i_row
        m_new = jnp.maximum(b_last + m_prev, jnp.max(g_s, axis=1, keepdims=True))
        w_s = jnp.exp(g_s - m_new)
        decay = jnp.exp(b_last + m_prev - m_new)
        c_ref[hd] = decay * c_ext + _dot((kT * w_s).astype(BF16), v_ext)
        m_ref[hd] = jnp.broadcast_to(m_new, m_ref.shape[1:])


def _mlstm_post_body(hf_ref, hb_ref, og_ref, ng_ref, w_ref, g_ref, h_ref, o_ref):
    H, dv = MLSTM_HEADS, MLSTM_V
    hs = hf_ref[0, 0] + hb_ref[0, 0]
    parts = []
    for hd in range(H):
        x = hs[:, hd * dv:(hd + 1) * dv]
        xc = x - jnp.mean(x, axis=1, keepdims=True)
        parts.append(xc * lax.rsqrt(jnp.mean(xc * xc, axis=1, keepdims=True) + EPS))
    y = jnp.concatenate(parts, axis=1) * ng_ref[...] * _sigmoid(og_ref[0].astype(F32))
    o_ref[0] = h_ref[0] + _rms(_dot(y.astype(BF16), w_ref[...]), g_ref[...])


def _mlstm_mixer(h, g0, w_in, b_gates, norm_g, w_out, g1, tm):
    B, S, D = h.shape
    H, dk, dv = MLSTM_HEADS, MLSTM_QK, MLSTM_V
    o1, o2, o3, o4 = H * dk, 2 * H * dk, 2 * H * dk + H * dv, 2 * H * dk + 2 * H * dv

    def pad_heads(w):
        w = w.reshape(D, H, dk)
        return jnp.concatenate([w, jnp.zeros((D, H, 128 - dk), w.dtype)], -1).reshape(D, H * 128)

    wq = pad_heads(w_in[:, :o1]).astype(BF16)
    wkT = pad_heads(w_in[:, o1:o2]).T.astype(BF16)
    wv = w_in[:, o2:o3].astype(BF16)
    wog = w_in[:, o3:o4].astype(BF16)
    wg = w_in[:, o4:].astype(BF16)
    q, v, og, g, kT, gT = _norm_proj(
        h, g0,
        [(wq, None, 1.0, BF16), (wv, None, 1.0, BF16), (wog, None, 1.0, BF16), (wg, b_gates, 1.0, F32)],
        [(wkT, None, dk ** -0.5, BF16), (wg.T, b_gates, 1.0, F32)], tm)
    g = jnp.moveaxis(g.reshape(B, S, 2, 2 * H), 2, 0)
    gT = jnp.moveaxis(gT.reshape(B, 2, 2 * H, S), 1, 0)
    NC = S // CHUNK

    def ck(dr, ci):
        return ci + dr * (NC - 1 - 2 * ci)

    hout = pl.pallas_call(
        _mlstm_body, grid=(2, B, NC),
        in_specs=[pl.BlockSpec((1, CHUNK, H * 128), lambda dr, b, ci: (b, ck(dr, ci), 0)),
                  pl.BlockSpec((1, H * 128, CHUNK), lambda dr, b, ci: (b, 0, ck(dr, ci))),
                  pl.BlockSpec((1, CHUNK, H * dv), lambda dr, b, ci: (b, ck(dr, ci), 0)),
                  pl.BlockSpec((1, 1, CHUNK, 2 * H), lambda dr, b, ci: (dr, b, ck(dr, ci), 0)),
                  pl.BlockSpec((1, 1, 2 * H, CHUNK), lambda dr, b, ci: (dr, b, 0, ck(dr, ci)))],
        out_specs=pl.BlockSpec((1, 1, CHUNK, H * dv), lambda dr, b, ci: (dr, b, ck(dr, ci), 0)),
        out_shape=jax.ShapeDtypeStruct((2, B, S, H * dv), F32),
        scratch_shapes=[pltpu.VMEM((H, 128, 2 * dv), F32), pltpu.VMEM((H, 1, 128), F32)],
        compiler_params=_params("parallel", "parallel", "arbitrary"), name="mlstm_scan",
    )(q, kT, v, g, gT)

    return pl.pallas_call(
        _mlstm_post_body, grid=(B, S // tm),
        in_specs=[pl.BlockSpec((1, 1, tm, H * dv), lambda b, i: (0, b, i, 0)),
                  pl.BlockSpec((1, 1, tm, H * dv), lambda b, i: (1, b, i, 0)),
                  pl.BlockSpec((1, tm, H * dv), lambda b, i: (b, i, 0)),
                  _resident((1, H * dv)), _resident((H * dv, D)), _resident((1, D)),
                  pl.BlockSpec((1, tm, D), lambda b, i: (b, i, 0))],
        out_specs=pl.BlockSpec((1, tm, D), lambda b, i: (b, i, 0)),
        out_shape=jax.ShapeDtypeStruct((B, S, D), F32),
        compiler_params=_params("parallel", "parallel"), name="mlstm_post",
    )(hout, hout, og, norm_g.reshape(1, -1), w_out.astype(BF16), g1.reshape(1, D), h)


def _ret_body(q_ref, kT_ref, v_ref, dl_ref, o_ref, r_ref):
    H, L, dk, dv = RET_HEADS, CHUNK, RET_QK, RET_V
    dr = pl.program_id(0)

    @pl.when(pl.program_id(2) == 0)
    def _():
        r_ref[...] = jnp.zeros_like(r_ref)

    r = lax.broadcasted_iota(jnp.int32, (L, L), 0)
    c = lax.broadcasted_iota(jnp.int32, (L, L), 1)
    rel = (r - c) * (1 - 2 * dr)
    relf = jnp.maximum(rel, 0).astype(F32)
    t_col = lax.broadcasted_iota(jnp.int32, (L, 1), 0)
    s_row = lax.broadcasted_iota(jnp.int32, (1, L), 1)
    xi_pow = (t_col + 1 + dr * (L - 1 - 2 * t_col)).astype(F32)
    zeta_pow = (L - 1 - s_row + dr * (2 * s_row - (L - 1))).astype(F32)
    lg = _log_sigmoid(dl_ref[0])
    for hd in range(H):
        lgh = lg[:, hd:hd + 1]
        dmask = jnp.where(rel >= 0, jnp.exp(relf * lgh), 0.0)
        qc = q_ref[0, :, hd * dk:(hd + 1) * dk]
        kT = kT_ref[0, hd * dk:(hd + 1) * dk, :]
        vc = v_ref[0, :, hd * dv:(hd + 1) * dv]
        state = r_ref[hd]
        inner = _dot((_dot(qc, kT) * dmask).astype(BF16), vc)
        cross = _dot(qc, state.astype(BF16)) * jnp.exp(xi_pow * lgh)
        o_ref[0, 0, :, hd * dv:(hd + 1) * dv] = inner + cross
        kz = (kT * jnp.exp(zeta_pow * lgh)).astype(BF16)
        r_ref[hd] = jnp.exp(L * lgh) * state + _dot(kz, vc)


def _ret_post_body(yf_ref, yb_ref, gate_ref, ng_ref, w_ref, g_ref, h_ref, o_ref):
    H, dv = RET_HEADS, RET_V
    ys = yf_ref[0, 0] + yb_ref[0, 0]
    parts = []
    for hd in range(H):
        x = ys[:, hd * dv:(hd + 1) * dv]
        xc = x - jnp.mean(x, axis=1, keepdims=True)
        parts.append(xc * lax.rsqrt(jnp.mean(xc * xc, axis=1, keepdims=True) + EPS))
    gate = gate_ref[0].astype(F32)
    y = jnp.concatenate(parts, axis=1) * ng_ref[...] * (gate * _sigmoid(gate))
    o_ref[0] = h_ref[0] + _rms(_dot(y.astype(BF16), w_ref[...]), g_ref[...])


def _ret_mixer(h, g0, w_in, decay_logit, norm_g, w_o, g1, tm):
    B, S, D = h.shape
    H, dk, dv = RET_HEADS, RET_QK, RET_V
    o1, o2, o3 = H * dk, 2 * H * dk, 2 * H * dk + H * dv
    wq = w_in[:, :o1].astype(BF16)
    wkT = w_in[:, o1:o2].T.astype(BF16)
    wv = w_in[:, o2:o3].astype(BF16)
    wgate = w_in[:, o3:].astype(BF16)
    q, v, gate, kT = _norm_proj(
        h, g0, [(wq, None, 1.0, BF16), (wv, None, 1.0, BF16), (wgate, None, 1.0, BF16)],
        [(wkT, None, dk ** -0.5, BF16)], tm)
    NC = S // CHUNK

    def ck(dr, ci):
        return ci + dr * (NC - 1 - 2 * ci)

    y = pl.pallas_call(
        _ret_body, grid=(2, B, NC),
        in_specs=[pl.BlockSpec((1, CHUNK, H * dk), lambda dr, b, ci: (b, ck(dr, ci), 0)),
                  pl.BlockSpec((1, H * dk, CHUNK), lambda dr, b, ci: (b, 0, ck(dr, ci))),
                  pl.BlockSpec((1, CHUNK, H * dv), lambda dr, b, ci: (b, ck(dr, ci), 0)),
                  pl.BlockSpec((1, 1, H), lambda dr, b, ci: (dr, 0, 0))],
        out_specs=pl.BlockSpec((1, 1, CHUNK, H * dv), lambda dr, b, ci: (dr, b, ck(dr, ci), 0)),
        out_shape=jax.ShapeDtypeStruct((2, B, S, H * dv), F32),
        scratch_shapes=[pltpu.VMEM((H, dk, dv), F32)],
        compiler_params=_params("parallel", "parallel", "arbitrary"), name="ret_scan",
    )(q, kT, v, decay_logit.astype(F32).reshape(2, 1, H))

    return pl.pallas_call(
        _ret_post_body, grid=(B, S // tm),
        in_specs=[pl.BlockSpec((1, 1, tm, H * dv), lambda b, i: (0, b, i, 0)),
                  pl.BlockSpec((1, 1, tm, H * dv), lambda b, i: (1, b, i, 0)),
                  pl.BlockSpec((1, tm, H * dv), lambda b, i: (b, i, 0)),
                  _resident((1, H * dv)), _resident((H * dv, D)), _resident((1, D)),
                  pl.BlockSpec((1, tm, D), lambda b, i: (b, i, 0))],
        out_specs=pl.BlockSpec((1, tm, D), lambda b, i: (b, i, 0)),
        out_shape=jax.ShapeDtypeStruct((B, S, D), F32),
        compiler_params=_params("parallel", "parallel"), name="ret_post",
    )(y, y, gate, norm_g.reshape(1, -1), w_o.astype(BF16), g1.reshape(1, D), h)


def _tile(s, want):
    return min(want, s)


def kernel(x, norm_g, ffn_w_up, ffn_conv_w, ffn_conv_b, ffn_w_down, mla_w_dq, mla_q_norm_g, mla_w_uq, mla_w_dkv, mla_kv_norm_g, mla_w_ukv, mla_w_o, diff_w_qkv, diff_lambda, diff_subln_g, diff_w_o, mlstm_w_in, mlstm_b_gates, mlstm_norm_g, mlstm_w_out, ret_w_in, ret_decay_logit, ret_norm_g, ret_w_o):
    depth = norm_g.shape[0]
    S = x.shape[1]
    tm = _tile(S, 512)
    mla_t = (_tile(S, MLA_TQ), _tile(S, MLA_TK), MLA_UNROLL)
    diff_t = (_tile(S, DIFF_TQ), _tile(S, DIFF_TK), DIFF_UNROLL)
    h = x
    for i in range(depth):
        kind, j = i % 4, i // 4
        g0, g1, g2, g3 = norm_g[i, 0], norm_g[i, 1], norm_g[i, 2], norm_g[i, 3]
        if kind == 0:
            h = _mla_mixer(h, g0, mla_w_dq[j], mla_q_norm_g[j], mla_w_uq[j], mla_w_dkv[j],
                           mla_kv_norm_g[j], mla_w_ukv[j], mla_w_o[j], g1, tm, *mla_t)
        elif kind == 1:
            h = _diff_mixer(h, g0, diff_w_qkv[j], diff_lambda[j], diff_subln_g[j], diff_w_o[j],
                            g1, i, tm, *diff_t)
        elif kind == 2:
            h = _mlstm_mixer(h, g0, mlstm_w_in[j], mlstm_b_gates[j], mlstm_norm_g[j],
                             mlstm_w_out[j], g1, tm)
        else:
            h = _ret_mixer(h, g0, ret_w_in[j], ret_decay_logit[j], ret_norm_g[j], ret_w_o[j], g1, tm)
        h = _conv_ffn(h, g2, ffn_w_up[i].astype(BF16), ffn_conv_w[i], ffn_conv_b[i],
                      ffn_w_down[i].astype(BF16), g3, tm)
    return h
```
